```python
import jax, jax.numpy as jnp
from jax import lax
import numpy as np

D_MODEL = 4096
BATCH = 16
SEQ = 2048
DEPTH = 1

N_META = 16
MIX_WIDTH = D_MODEL
CONV_CH = MIX_WIDTH // 2
POOL_CH = MIX_WIDTH - CONV_CH
CONV_HEADS = 16
CONV_K = 3
POOL_WINDOWS = (2, 4, 8, 16)
N_POOL_GROUPS = len(POOL_WINDOWS)
POOL_GROUP = POOL_CH // N_POOL_GROUPS
IN_COLS = 3 * CONV_CH + POOL_CH
D_FF = 256 * ((8 * D_MODEL // 3 + 255) // 256)
LN_EPS = 1e-5
ALPHA = (2.0 * DEPTH) ** 0.25
BETA = (8.0 * DEPTH) ** -0.25

kernel_name = "hybrid_conv_pool_macaron_deepnorm"


def layer_norm(x, g, b):
    xf = x.astype(jnp.float32)
    mu = jnp.mean(xf, axis=-1, keepdims=True)
    xc = xf - mu
    var = jnp.mean(jnp.square(xc), axis=-1, keepdims=True)
    y = xc * lax.rsqrt(var + LN_EPS) * g.astype(jnp.float32) + b.astype(jnp.float32)
    return y.astype(x.dtype)


def swiglu_ffn(x, w_gu, w_down):
    gu = jnp.einsum('bld,df->blf', x, w_gu)
    gate, up = jnp.split(gu, 2, axis=-1)
    return jnp.einsum('blf,fd->bld', jax.nn.silu(gate) * up, w_down)


def causal_short_conv(z, w):
    L = z.shape[1]
    zp = jnp.pad(z, ((0, 0), (CONV_K - 1, 0), (0, 0)))
    y = zp[:, 0:L] * w[0]
    for k in range(1, CONV_K):
        y = y + zp[:, k:k + L] * w[k]
    return y


def causal_window_mean(z, window):
    L = z.shape[1]
    cs = jnp.cumsum(z, axis=1)
    prev = jnp.pad(cs, ((0, 0), (window, 0), (0, 0)))[:, :L]
    count = jnp.minimum(jnp.arange(1, L + 1), window).astype(jnp.float32)
    return (cs - prev) / count[None, :, None]


def pooling_mixer(z, pool_w, pool_scale):
    b, L, _ = z.shape
    zg = z.reshape(b, L, N_POOL_GROUPS, POOL_GROUP).astype(jnp.float32)
    pooled = jnp.stack([causal_window_mean(zg[:, :, g], POOL_WINDOWS[g])
                        for g in range(N_POOL_GROUPS)], axis=2)
    d = (pooled - zg).astype(z.dtype)
    y = jnp.einsum('blgc,gcd->blgd', d, pool_w).reshape(b, L, POOL_CH)
    return y * pool_scale


def hybrid_mixer(h, w_in, conv_w, pool_w, pool_scale, w_out):
    u = jnp.einsum('bld,dc->blc', h, w_in)
    gate_b = u[..., 0:CONV_CH]
    gate_c = u[..., CONV_CH:2 * CONV_CH]
    x_in = u[..., 2 * CONV_CH:3 * CONV_CH]
    z_pool = u[..., 3 * CONV_CH:]
    y_conv = gate_b * causal_short_conv(gate_c * x_in, conv_w)
    y_pool = pooling_mixer(z_pool, pool_w, pool_scale)
    y = jnp.concatenate([y_conv, y_pool], axis=-1)
    return jnp.einsum('blc,cd->bld', y, w_out)


def _fwd_setup_inputs(seed: int = 0) -> dict:
    key = jax.random.key(seed)
    ks = jax.random.split(key, 20)
    f32 = jnp.float32
    D, F = D_MODEL, D_FF

    def nrm(k, shape, scale):
        return jax.random.normal(k, shape, f32) * scale

    def gain(k):
        return 1.0 + 0.05 * jax.random.normal(k, (DEPTH, D), f32)

    def bias(k):
        return 0.02 * jax.random.normal(k, (DEPTH, D), f32)

    return {
        "x": jax.random.normal(ks[0], (BATCH, SEQ, D), f32),
        "meta_tokens": nrm(ks[1], (N_META, D), 1.0),
        "ffn1_w_gu": nrm(ks[2], (DEPTH, D, 2 * F), D ** -0.5),
        "ffn1_w_down": nrm(ks[3], (DEPTH, F, D), BETA * F ** -0.5),
        "ln1_g": gain(ks[4]),
        "ln1_b": bias(ks[5]),
        "w_in": nrm(ks[6], (DEPTH, D, IN_COLS), D ** -0.5),
        "conv_w": nrm(ks[7], (DEPTH, CONV_K, CONV_CH), CONV_K ** -0.5),
        "pool_w": nrm(ks[8], (DEPTH, N_POOL_GROUPS, POOL_GROUP, POOL_GROUP), POOL_GROUP ** -0.5),
        "pool_scale": 1.0 + 0.1 * jax.random.normal(ks[9], (DEPTH, POOL_CH), f32),
        "w_out": nrm(ks[10], (DEPTH, MIX_WIDTH, D), BETA * MIX_WIDTH ** -0.5),
        "ln2_g": gain(ks[11]),
        "ln2_b": bias(ks[12]),
        "ffn2_w_gu": nrm(ks[13], (DEPTH, D, 2 * F), D ** -0.5),
        "ffn2_w_down": nrm(ks[14], (DEPTH, F, D), BETA * F ** -0.5),
        "ln3_g": gain(ks[15]),
        "ln3_b": bias(ks[16]),
    }


def _fwd_reference(x, meta_tokens, ffn1_w_gu, ffn1_w_down, ln1_g, ln1_b, w_in, conv_w, pool_w,
              pool_scale, w_out, ln2_g, ln2_b, ffn2_w_gu, ffn2_w_down, ln3_g, ln3_b):
    b = x.shape[0]
    meta = jnp.broadcast_to(meta_tokens.astype(x.dtype)[None], (b, N_META, D_MODEL))
    h = jnp.concatenate([meta, x], axis=1)
    for i in range(DEPTH):
        h = layer_norm(ALPHA * h + 0.5 * swiglu_ffn(h, ffn1_w_gu[i], ffn1_w_down[i]),
                       ln1_g[i], ln1_b[i])
        h = layer_norm(ALPHA * h + hybrid_mixer(h, w_in[i], conv_w[i], pool_w[i],
                                                 pool_scale[i], w_out[i]),
                       ln2_g[i], ln2_b[i])
        h = layer_norm(ALPHA * h + 0.5 * swiglu_ffn(h, ffn2_w_gu[i], ffn2_w_down[i]),
                       ln3_g[i], ln3_b[i])
    return h[:, N_META:]


import jax as _jax
import jax.numpy as _jnp

TWIN_FORMAT = 'train_step'
FWD_PARAMS = ['x', 'meta_tokens', 'ffn1_w_gu', 'ffn1_w_down', 'ln1_g', 'ln1_b', 'w_in', 'conv_w', 'pool_w', 'pool_scale', 'w_out', 'ln2_g', 'ln2_b', 'ffn2_w_gu', 'ffn2_w_down', 'ln3_g', 'ln3_b']
TWIN_WEIGHTS = ['meta_tokens', 'ffn1_w_gu', 'ffn1_w_down', 'ln1_g', 'ln1_b', 'w_in', 'conv_w', 'pool_w', 'pool_scale', 'w_out', 'ln2_g', 'ln2_b', 'ffn2_w_gu', 'ffn2_w_down', 'ln3_g', 'ln3_b']
TWIN_DIFF_INPUT = 'x'
TWIN_INPUTS = ['x', 'meta_tokens', 'ffn1_w_gu', 'ffn1_w_down', 'ln1_g', 'ln1_b', 'w_in', 'conv_w', 'pool_w', 'pool_scale', 'w_out', 'ln2_g', 'ln2_b', 'ffn2_w_gu', 'ffn2_w_down', 'ln3_g', 'ln3_b', 'loss_target', 'm_meta_tokens', 'm_ffn1_w_gu', 'm_ffn1_w_down', 'm_ln1_g', 'm_ln1_b', 'm_w_in', 'm_conv_w', 'm_pool_w', 'm_pool_scale', 'm_w_out', 'm_ln2_g', 'm_ln2_b', 'm_ffn2_w_gu', 'm_ffn2_w_down', 'm_ln3_g', 'm_ln3_b', 'v_meta_tokens', 'v_ffn1_w_gu', 'v_ffn1_w_down', 'v_ln1_g', 'v_ln1_b', 'v_w_in', 'v_conv_w', 'v_pool_w', 'v_pool_scale', 'v_w_out', 'v_ln2_g', 'v_ln2_b', 'v_ffn2_w_gu', 'v_ffn2_w_down', 'v_ln3_g', 'v_ln3_b']
TWIN_OUTPUTS = ['loss', 'grad_x', 'grad_meta_tokens', 'grad_ffn1_w_gu', 'grad_ffn1_w_down', 'grad_ln1_g', 'grad_ln1_b', 'grad_w_in', 'grad_conv_w', 'grad_pool_w', 'grad_pool_scale', 'grad_w_out', 'grad_ln2_g', 'grad_ln2_b', 'grad_ffn2_w_gu', 'grad_ffn2_w_down', 'grad_ln3_g', 'grad_ln3_b', 'delta_meta_tokens', 'delta_ffn1_w_gu', 'delta_ffn1_w_down', 'delta_ln1_g', 'delta_ln1_b', 'delta_w_in', 'delta_conv_w', 'delta_pool_w', 'delta_pool_scale', 'delta_w_out', 'delta_ln2_g', 'delta_ln2_b', 'delta_ffn2_w_gu', 'delta_ffn2_w_down', 'delta_ln3_g', 'delta_ln3_b', 'new_m_meta_tokens', 'new_m_ffn1_w_gu', 'new_m_ffn1_w_down', 'new_m_ln1_g', 'new_m_ln1_b', 'new_m_w_in', 'new_m_conv_w', 'new_m_pool_w', 'new_m_pool_scale', 'new_m_w_out', 'new_m_ln2_g', 'new_m_ln2_b', 'new_m_ffn2_w_gu', 'new_m_ffn2_w_down', 'new_m_ln3_g', 'new_m_ln3_b', 'new_v_meta_tokens', 'new_v_ffn1_w_gu', 'new_v_ffn1_w_down', 'new_v_ln1_g', 'new_v_ln1_b', 'new_v_w_in', 'new_v_conv_w', 'new_v_pool_w', 'new_v_pool_scale', 'new_v_w_out', 'new_v_ln2_g', 'new_v_ln2_b', 'new_v_ffn2_w_gu', 'new_v_ffn2_w_down', 'new_v_ln3_g', 'new_v_ln3_b']
TWIN_LEAF_KINDS = {'loss': 'loss', 'grad_x': 'grad_x', 'grad_meta_tokens': 'grad_w', 'grad_ffn1_w_gu': 'grad_w', 'grad_ffn1_w_down': 'grad_w', 'grad_ln1_g': 'grad_w', 'grad_ln1_b': 'grad_w', 'grad_w_in': 'grad_w', 'grad_conv_w': 'grad_w', 'grad_pool_w': 'grad_w', 'grad_pool_scale': 'grad_w', 'grad_w_out': 'grad_w', 'grad_ln2_g': 'grad_w', 'grad_ln2_b': 'grad_w', 'grad_ffn2_w_gu': 'grad_w', 'grad_ffn2_w_down': 'grad_w', 'grad_ln3_g': 'grad_w', 'grad_ln3_b': 'grad_w', 'delta_meta_tokens': 'delta_w', 'delta_ffn1_w_gu': 'delta_w', 'delta_ffn1_w_down': 'delta_w', 'delta_ln1_g': 'delta_w', 'delta_ln1_b': 'delta_w', 'delta_w_in': 'delta_w', 'delta_conv_w': 'delta_w', 'delta_pool_w': 'delta_w', 'delta_pool_scale': 'delta_w', 'delta_w_out': 'delta_w', 'delta_ln2_g': 'delta_w', 'delta_ln2_b': 'delta_w', 'delta_ffn2_w_gu': 'delta_w', 'delta_ffn2_w_down': 'delta_w', 'delta_ln3_g': 'delta_w', 'delta_ln3_b': 'delta_w', 'new_m_meta_tokens': 'new_m', 'new_m_ffn1_w_gu': 'new_m', 'new_m_ffn1_w_down': 'new_m', 'new_m_ln1_g': 'new_m', 'new_m_ln1_b': 'new_m', 'new_m_w_in': 'new_m', 'new_m_conv_w': 'new_m', 'new_m_pool_w': 'new_m', 'new_m_pool_scale': 'new_m', 'new_m_w_out': 'new_m', 'new_m_ln2_g': 'new_m', 'new_m_ln2_b': 'new_m', 'new_m_ffn2_w_gu': 'new_m', 'new_m_ffn2_w_down': 'new_m', 'new_m_ln3_g': 'new_m', 'new_m_ln3_b': 'new_m', 'new_v_meta_tokens': 'new_v', 'new_v_ffn1_w_gu': 'new_v', 'new_v_ffn1_w_down': 'new_v', 'new_v_ln1_g': 'new_v', 'new_v_ln1_b': 'new_v', 'new_v_w_in': 'new_v', 'new_v_conv_w': 'new_v', 'new_v_pool_w': 'new_v', 'new_v_pool_scale': 'new_v', 'new_v_w_out': 'new_v', 'new_v_ln2_g': 'new_v', 'new_v_ln2_b': 'new_v', 'new_v_ffn2_w_gu': 'new_v', 'new_v_ffn2_w_down': 'new_v', 'new_v_ln3_g': 'new_v', 'new_v_ln3_b': 'new_v'}


def _forward(args):
    return _fwd_reference(*[args[k] for k in FWD_PARAMS])


def _output_shape():
    def fwd():
        inp = _fwd_setup_inputs(0)
        return _fwd_reference(*[inp[k] for k in FWD_PARAMS])
    out = _jax.eval_shape(fwd)
    return out.shape, out.dtype

N_MICROBATCH = 1
ADAM_LR = 0.001
ADAM_B1 = 0.9
ADAM_B2 = 0.999
ADAM_EPS = 1e-08
ADAM_WD = 0.01
ADAM_STEP = 10
PER_EXAMPLE_BATCH_AXIS = {'x': 0, 'loss_target': 0}
SHARED_INPUTS = []
_WEIGHT_DTYPES = {'meta_tokens': _jnp.float32, 'ffn1_w_gu': _jnp.float32, 'ffn1_w_down': _jnp.float32, 'ln1_g': _jnp.float32, 'ln1_b': _jnp.float32, 'w_in': _jnp.float32, 'conv_w': _jnp.float32, 'pool_w': _jnp.float32, 'pool_scale': _jnp.float32, 'w_out': _jnp.float32, 'ln2_g': _jnp.float32, 'ln2_b': _jnp.float32, 'ffn2_w_gu': _jnp.float32, 'ffn2_w_down': _jnp.float32, 'ln3_g': _jnp.float32, 'ln3_b': _jnp.float32}
MOMENT_SCALE = {'meta_tokens': 3.484105e-04, 'ffn1_w_gu': 4.515311e-03, 'ffn1_w_down': 1.225531e-02, 'ln1_g': 6.241680e-01, 'ln1_b': 1.276223e-01, 'w_in': 1.976494e-02, 'conv_w': 2.060433e-02, 'pool_w': 1.810085e-02, 'pool_scale': 1.801578e-02, 'w_out': 3.242705e-02, 'ln2_g': 7.674088e-01, 'ln2_b': 1.432996e-01, 'ffn2_w_gu': 4.078821e-03, 'ffn2_w_down': 1.108172e-02, 'ln3_g': 8.067984e+00, 'ln3_b': 2.294101e-01}


def _to_microbatches(a, axis):
    t = _jnp.moveaxis(a, axis, 0)
    t = t.reshape((N_MICROBATCH, t.shape[0] // N_MICROBATCH) + t.shape[1:])
    return _jnp.moveaxis(t, 1, axis + 1)


def setup_inputs(seed: int = 0) -> dict:
    inp = _fwd_setup_inputs(seed)
    key = _jax.random.fold_in(_jax.random.key(seed), 7919)
    shape, _ = _output_shape()
    out = dict(inp)
    out["loss_target"] = _jax.random.normal(_jax.random.fold_in(key, 0), shape, _jnp.float32)
    for i, name in enumerate(TWIN_WEIGHTS):
        w = inp[name].astype(_jnp.float32)
        if MOMENT_SCALE is None:
            s = _jnp.sqrt(_jnp.mean(_jnp.square(w)) + 1e-30)
        else:
            s = MOMENT_SCALE[name]
        km, kv = _jax.random.split(_jax.random.fold_in(key, i + 1))
        out[name] = w
        out["m_" + name] = s * _jax.random.normal(km, w.shape, _jnp.float32)
        out["v_" + name] = (s * s) * _jax.random.uniform(kv, w.shape, _jnp.float32, 0.5, 1.5)
    if N_MICROBATCH > 1:
        for name, axis in PER_EXAMPLE_BATCH_AXIS.items():
            out[name] = _to_microbatches(out[name], axis)
    return {'x': out['x'], 'meta_tokens': out['meta_tokens'], 'ffn1_w_gu': out['ffn1_w_gu'], 'ffn1_w_down': out['ffn1_w_down'], 'ln1_g': out['ln1_g'], 'ln1_b': out['ln1_b'], 'w_in': out['w_in'], 'conv_w': out['conv_w'], 'pool_w': out['pool_w'], 'pool_scale': out['pool_scale'], 'w_out': out['w_out'], 'ln2_g': out['ln2_g'], 'ln2_b': out['ln2_b'], 'ffn2_w_gu': out['ffn2_w_gu'], 'ffn2_w_down': out['ffn2_w_down'], 'ln3_g': out['ln3_g'], 'ln3_b': out['ln3_b'], 'loss_target': out['loss_target'], 'm_meta_tokens': out['m_meta_tokens'], 'm_ffn1_w_gu': out['m_ffn1_w_gu'], 'm_ffn1_w_down': out['m_ffn1_w_down'], 'm_ln1_g': out['m_ln1_g'], 'm_ln1_b': out['m_ln1_b'], 'm_w_in': out['m_w_in'], 'm_conv_w': out['m_conv_w'], 'm_pool_w': out['m_pool_w'], 'm_pool_scale': out['m_pool_scale'], 'm_w_out': out['m_w_out'], 'm_ln2_g': out['m_ln2_g'], 'm_ln2_b': out['m_ln2_b'], 'm_ffn2_w_gu': out['m_ffn2_w_gu'], 'm_ffn2_w_down': out['m_ffn2_w_down'], 'm_ln3_g': out['m_ln3_g'], 'm_ln3_b': out['m_ln3_b'], 'v_meta_tokens': out['v_meta_tokens'], 'v_ffn1_w_gu': out['v_ffn1_w_gu'], 'v_ffn1_w_down': out['v_ffn1_w_down'], 'v_ln1_g': out['v_ln1_g'], 'v_ln1_b': out['v_ln1_b'], 'v_w_in': out['v_w_in'], 'v_conv_w': out['v_conv_w'], 'v_pool_w': out['v_pool_w'], 'v_pool_scale': out['v_pool_scale'], 'v_w_out': out['v_w_out'], 'v_ln2_g': out['v_ln2_g'], 'v_ln2_b': out['v_ln2_b'], 'v_ffn2_w_gu': out['v_ffn2_w_gu'], 'v_ffn2_w_down': out['v_ffn2_w_down'], 'v_ln3_g': out['v_ln3_g'], 'v_ln3_b': out['v_ln3_b']}


def _loss(weights, diff, rest, loss_target):
    with _jax.named_scope("forward"):
        args = {**rest, TWIN_DIFF_INPUT: diff, **{k: w.astype(_WEIGHT_DTYPES[k]) for k, w in weights.items()}}
        y = _forward(args)
    with _jax.named_scope("loss_head"):
        err = _jnp.square(y.astype(_jnp.float32) - loss_target)
        return 0.5 * _jnp.sum(_jnp.mean(err, axis=-1)) if err.ndim else 0.5 * err


def _adamw(w, g, m, v):
    m = ADAM_B1 * m + (1.0 - ADAM_B1) * g
    v = ADAM_B2 * v + (1.0 - ADAM_B2) * _jnp.square(g)
    m_hat = m / (1.0 - ADAM_B1 ** ADAM_STEP)
    v_hat = v / (1.0 - ADAM_B2 ** ADAM_STEP)
    delta = -ADAM_LR * (m_hat / (_jnp.sqrt(v_hat) + ADAM_EPS) + ADAM_WD * w)
    return delta, m, v


def reference(x, meta_tokens, ffn1_w_gu, ffn1_w_down, ln1_g, ln1_b, w_in, conv_w, pool_w, pool_scale, w_out, ln2_g, ln2_b, ffn2_w_gu, ffn2_w_down, ln3_g, ln3_b, loss_target, m_meta_tokens, m_ffn1_w_gu, m_ffn1_w_down, m_ln1_g, m_ln1_b, m_w_in, m_conv_w, m_pool_w, m_pool_scale, m_w_out, m_ln2_g, m_ln2_b, m_ffn2_w_gu, m_ffn2_w_down, m_ln3_g, m_ln3_b, v_meta_tokens, v_ffn1_w_gu, v_ffn1_w_down, v_ln1_g, v_ln1_b, v_w_in, v_conv_w, v_pool_w, v_pool_scale, v_w_out, v_ln2_g, v_ln2_b, v_ffn2_w_gu, v_ffn2_w_down, v_ln3_g, v_ln3_b):
    given = dict(x=x, meta_tokens=meta_tokens, ffn1_w_gu=ffn1_w_gu, ffn1_w_down=ffn1_w_down, ln1_g=ln1_g, ln1_b=ln1_b, w_in=w_in, conv_w=conv_w, pool_w=pool_w, pool_scale=pool_scale, w_out=w_out, ln2_g=ln2_g, ln2_b=ln2_b, ffn2_w_gu=ffn2_w_gu, ffn2_w_down=ffn2_w_down, ln3_g=ln3_g, ln3_b=ln3_b, loss_target=loss_target, m_meta_tokens=m_meta_tokens, m_ffn1_w_gu=m_ffn1_w_gu, m_ffn1_w_down=m_ffn1_w_down, m_ln1_g=m_ln1_g, m_ln1_b=m_ln1_b, m_w_in=m_w_in, m_conv_w=m_conv_w, m_pool_w=m_pool_w, m_pool_scale=m_pool_scale, m_w_out=m_w_out, m_ln2_g=m_ln2_g, m_ln2_b=m_ln2_b, m_ffn2_w_gu=m_ffn2_w_gu, m_ffn2_w_down=m_ffn2_w_down, m_ln3_g=m_ln3_g, m_ln3_b=m_ln3_b, v_meta_tokens=v_meta_tokens, v_ffn1_w_gu=v_ffn1_w_gu, v_ffn1_w_down=v_ffn1_w_down, v_ln1_g=v_ln1_g, v_ln1_b=v_ln1_b, v_w_in=v_w_in, v_conv_w=v_conv_w, v_pool_w=v_pool_w, v_pool_scale=v_pool_scale, v_w_out=v_w_out, v_ln2_g=v_ln2_g, v_ln2_b=v_ln2_b, v_ffn2_w_gu=v_ffn2_w_gu, v_ffn2_w_down=v_ffn2_w_down, v_ln3_g=v_ln3_g, v_ln3_b=v_ln3_b)
    weights = {n: given[n] for n in TWIN_WEIGHTS}
    shared = {n: given[n] for n in SHARED_INPUTS}
    per_example = {n: given[n] for n in ['x']}
    grad_fn = _jax.value_and_grad(_loss, argnums=(0, 1))

    def one_microbatch(ex, loss_target):
        ex = dict(ex)
        diff = ex.pop(TWIN_DIFF_INPUT)
        return grad_fn(weights, diff, {**shared, **ex}, loss_target)

    if N_MICROBATCH == 1:
        loss, (grad_w, grad_x) = one_microbatch(per_example, given["loss_target"])
    else:
        def body(carry, xs):
            loss_sum, grad_sum = carry
            l_k, (gw_k, gx_k) = one_microbatch(xs[0], xs[1])
            with _jax.named_scope("update"):
                return (loss_sum + l_k, _jax.tree.map(_jnp.add, grad_sum, gw_k)), gx_k

        init = (_jnp.zeros((), _jnp.float32), _jax.tree.map(_jnp.zeros_like, weights))
        (loss, grad_w), grad_x = _jax.lax.scan(body, init, (per_example, given["loss_target"]))
    with _jax.named_scope("update"):
        delta_w, new_m, new_v = {}, {}, {}
        for n in TWIN_WEIGHTS:
            delta_w[n], new_m[n], new_v[n] = _adamw(weights[n], grad_w[n], given["m_" + n], given["v_" + n])
    return (loss, grad_x, *[grad_w[n] for n in TWIN_WEIGHTS], *[delta_w[n] for n in TWIN_WEIGHTS],
            *[new_m[n] for n in TWIN_WEIGHTS], *[new_v[n] for n in TWIN_WEIGHTS])
```

```python
import jax
import jax.numpy as jnp
from jax import lax
from jax.experimental import pallas as pl
from jax.experimental.pallas import tpu as pltpu

F32 = jnp.float32
BF16 = jnp.bfloat16
MESH = pl.DeviceIdType.MESH
ALL_AXES = ("x", "y", "c")

N_META = 16
POOL_WINDOWS = (2, 4, 8, 16)
LN_EPS = 1e-5
ALPHA = 2.0 ** 0.25
ADAM_LR, ADAM_B1, ADAM_B2, ADAM_EPS, ADAM_WD, ADAM_STEP = 0.001, 0.9, 0.999, 1e-08, 0.01, 10

VMEM_LIMIT_V7X = 60 * 1024 * 1024
LANES = 128
ELEMENTWISE_BLOCK_BYTES = 2 * 1024 * 1024


def _params(**kw):
    return pltpu.CompilerParams(vmem_limit_bytes=VMEM_LIMIT_V7X, **kw)


def _tile(n, prefs):
    for t in prefs:
        if t <= n and n % t == 0:
            return t
    return n


def _rows_tile(rows, cols, mult=16):
    cap = max(mult, ELEMENTWISE_BLOCK_BYTES // (4 * cols))
    best = None
    for t in range(mult, min(rows, cap) + 1, mult):
        if rows % t == 0:
            best = t
    return best if best is not None else rows


def _matmul(a, b, *, mode, tm, tn, tk, out_dtype, name, a_outer=True, scale=1.0):
    if mode == "tn":
        K, M = a.shape
    else:
        M, K = a.shape
    N = b.shape[0] if mode == "nt" else b.shape[1]
    assert M % tm == 0 and N % tn == 0 and K % tk == 0, (name, M, N, K, tm, tn, tk)
    nk = K // tk
    if a_outer:
        grid = (M // tm, N // tn, nk)
        ij = lambda p, q: (p, q)
    else:
        grid = (N // tn, M // tm, nk)
        ij = lambda p, q: (q, p)

    if mode == "tn":
        a_spec = pl.BlockSpec((tk, tm), lambda p, q, k: (k, ij(p, q)[0]))
        dims = (((0,), (0,)), ((), ()))
    else:
        a_spec = pl.BlockSpec((tm, tk), lambda p, q, k: (ij(p, q)[0], k))
        dims = (((1,), (1,)), ((), ())) if mode == "nt" else (((1,), (0,)), ((), ()))
    if mode == "nt":
        b_spec = pl.BlockSpec((tn, tk), lambda p, q, k: (ij(p, q)[1], k))
    else:
        b_spec = pl.BlockSpec((tk, tn), lambda p, q, k: (k, ij(p, q)[1]))
    o_spec = pl.BlockSpec((tm, tn), lambda p, q, k: ij(p, q))

    def fin(acc):
        if scale != 1.0:
            acc = acc * scale
        return acc.astype(out_dtype)

    if nk == 1:
        def body(a_ref, b_ref, o_ref):
            o_ref[...] = fin(lax.dot_general(a_ref[...], b_ref[...], dims, preferred_element_type=F32))
        scratch = []
    else:
        def body(a_ref, b_ref, o_ref, acc_ref):
            k = pl.program_id(2)
            prod = lax.dot_general(a_ref[...], b_ref[...], dims, preferred_element_type=F32)

            @pl.when(k == 0)
            def _():
                acc_ref[...] = prod

            @pl.when(k > 0)
            def _():
                acc_ref[...] += prod

            @pl.when(k == nk - 1)
            def _():
                o_ref[...] = fin(acc_ref[...])
        scratch = [pltpu.VMEM((tm, tn), F32)]

    return pl.pallas_call(
        body, name=name, grid=grid, in_specs=[a_spec, b_spec], out_specs=o_spec,
        out_shape=jax.ShapeDtypeStruct((M, N), out_dtype), scratch_shapes=scratch,
        compiler_params=_params(),
    )(a, b)


TM_PREFS = (1376, 688, 512, 256, 128, 64)


def _mm_nn(a, b, out_dtype, name):
    M, K = a.shape
    N = b.shape[1]
    big_k = K * 2 * 1376 > 12 * 1024 * 1024
    tm = _tile(M, (688,) + TM_PREFS[2:]) if big_k else _tile(M, TM_PREFS)
    tn = _tile(N, (256, 128)) if big_k else _tile(N, (512, 256, 128))
    return _matmul(a, b, mode="nn", tm=tm, tn=tn, tk=K, out_dtype=out_dtype, name=name)


def _mm_nt(a, b, out_dtype, name, scale=1.0):
    M, K = a.shape
    N = b.shape[0]
    tm = _tile(M, TM_PREFS)
    if K * 2 * tm > 12 * 1024 * 1024:
        tk = _tile(K, (2048, 512, 256, 128))
        tn = _tile(N, (2048,) if tk <= 512 else (1024, 512, 256, 128))
    else:
        tk = K
        tn = _tile(N, (512, 256, 128))
    return _matmul(a, b, mode="nt", tm=tm, tn=tn, tk=tk, out_dtype=out_dtype, name=name, scale=scale)


def _mm_tn(a, b, name):
    T, M = a.shape
    N = b.shape[1]
    if M % 2048 == 0:
        tm, tn, a_outer = 2048, _tile(N, (512, 256, 128)), True
    else:
        tm, tn, a_outer = _tile(M, (256, 128)), _tile(N, (2048, 1024, 512, 256, 128)), False
    return _matmul(a, b, mode="tn", tm=tm, tn=tn, tk=T, out_dtype=BF16, name=name, a_outer=a_outer)


def _silu_mul(gu, name):
    T, F2 = gu.shape
    F = F2 // 2
    tm = _rows_tile(T, F)

    def body(g_ref, u_ref, o_ref):
        g = g_ref[...].astype(F32)
        o_ref[...] = (g * jax.nn.sigmoid(g) * u_ref[...].astype(F32)).astype(BF16)

    return pl.pallas_call(
        body, name=name, grid=(T // tm,),
        in_specs=[pl.BlockSpec((tm, F), lambda i: (i, 0)), pl.BlockSpec((tm, F), lambda i: (i, 1))],
        out_specs=pl.BlockSpec((tm, F), lambda i: (i, 0)),
        out_shape=jax.ShapeDtypeStruct((T, F), BF16), compiler_params=_params(),
    )(gu, gu)


def _swiglu_bwd(da, gu, name):
    T, F2 = gu.shape
    F = F2 // 2
    tm = _rows_tile(T, F)

    def body(da_ref, g_ref, u_ref, o_ref):
        half = pl.program_id(1)
        g = g_ref[...].astype(F32)
        da_ = da_ref[...].astype(F32)
        s = jax.nn.sigmoid(g)

        @pl.when(half == 0)
        def _():
            o_ref[...] = (da_ * u_ref[...].astype(F32) * (s * (1.0 + g * (1.0 - s)))).astype(BF16)

        @pl.when(half == 1)
        def _():
            o_ref[...] = (da_ * (g * s)).astype(BF16)

    return pl.pallas_call(
        body, name=name, grid=(T // tm, 2),
        in_specs=[pl.BlockSpec((tm, F), lambda i, h: (i, 0)), pl.BlockSpec((tm, F), lambda i, h: (i, 0)),
                  pl.BlockSpec((tm, F), lambda i, h: (i, 1))],
        out_specs=pl.BlockSpec((tm, F), lambda i, h: (i, h)),
        out_shape=jax.ShapeDtypeStruct((T, F2), BF16), compiler_params=_params(),
    )(da, gu, gu)


def _ln_stats(r):
    mu = jnp.mean(r, axis=-1, keepdims=True)
    xc = r - mu
    var = jnp.mean(xc * xc, axis=-1, keepdims=True)
    rstd = lax.rsqrt(var + LN_EPS)
    return xc * rstd, rstd


def _ln_bwd_math(dh, xhat, rstd, g):
    dxh = dh * g
    m1 = jnp.mean(dxh, axis=-1, keepdims=True)
    m2 = jnp.mean(dxh * xhat, axis=-1, keepdims=True)
    return rstd * (dxh - m1 - xhat * m2)


def _ln_fwd(hprev, f, g, b, fscale, name):
    T, D = hprev.shape
    tm = _rows_tile(T, D)

    def body(hp_ref, f_ref, g_ref, b_ref, h_ref, hb_ref, xh_ref, rs_ref):
        r = ALPHA * hp_ref[...] + fscale * f_ref[...]
        xhat, rstd = _ln_stats(r)
        h = xhat * g_ref[...] + b_ref[...]
        h_ref[...] = h
        hb_ref[...] = h.astype(BF16)
        xh_ref[...] = xhat
        rs_ref[...] = rstd

    row = pl.BlockSpec((tm, D), lambda i: (i, 0))
    vec = pl.BlockSpec((1, D), lambda i: (0, 0))
    col = pl.BlockSpec((tm, 1), lambda i: (i, 0))
    return pl.pallas_call(
        body, name=name, grid=(T // tm,), in_specs=[row, row, vec, vec], out_specs=[row, row, row, col],
        out_shape=[jax.ShapeDtypeStruct((T, D), F32), jax.ShapeDtypeStruct((T, D), BF16),
                   jax.ShapeDtypeStruct((T, D), F32), jax.ShapeDtypeStruct((T, 1), F32)],
        compiler_params=_params(),
    )(hprev, f, g, b)


def _ln_bwd(dr_next, dh_branch, xhat, rstd, g, fscale, name):
    T, D = xhat.shape
    tm = _rows_tile(T, D)

    def body(dn_ref, db_ref, xh_ref, rs_ref, g_ref, dr_ref, drb_ref, gs_ref, bs_ref):
        i = pl.program_id(0)
        dh = ALPHA * dn_ref[...] + db_ref[...]
        xhat_ = xh_ref[...]
        dr = _ln_bwd_math(dh, xhat_, rs_ref[...], g_ref[...])
        dr_ref[...] = dr
        drb_ref[...] = (fscale * dr).astype(BF16)

        @pl.when(i == 0)
        def _():
            gs_ref[...] = jnp.zeros_like(gs_ref)
            bs_ref[...] = jnp.zeros_like(bs_ref)

        gs_ref[...] += jnp.sum(dh * xhat_, axis=0, keepdims=True)
        bs_ref[...] += jnp.sum(dh, axis=0, keepdims=True)

    row = pl.BlockSpec((tm, D), lambda i: (i, 0))
    vec = pl.BlockSpec((1, D), lambda i: (0, 0))
    col = pl.BlockSpec((tm, 1), lambda i: (i, 0))
    return pl.pallas_call(
        body, name=name, grid=(T // tm,), in_specs=[row, row, row, col, vec], out_specs=[row, row, vec, vec],
        out_shape=[jax.ShapeDtypeStruct((T, D), F32), jax.ShapeDtypeStruct((T, D), BF16),
                   jax.ShapeDtypeStruct((1, D), F32), jax.ShapeDtypeStruct((1, D), F32)],
        compiler_params=_params(),
    )(dr_next, dh_branch, xhat, rstd, g)


def _ln3_loss(hprev, f, target, mask, g, b, name):
    T, D = hprev.shape
    tm = _rows_tile(T, D)

    def body(hp_ref, f_ref, t_ref, m_ref, g_ref, b_ref, dr_ref, drb_ref, gs_ref, bs_ref, ls_ref):
        i = pl.program_id(0)
        r = ALPHA * hp_ref[...] + 0.5 * f_ref[...]
        xhat, rstd = _ln_stats(r)
        g_ = g_ref[...]
        y = xhat * g_ + b_ref[...]
        err = (y - t_ref[...]) * m_ref[...]
        dy = err * (1.0 / D)
        dr = _ln_bwd_math(dy, xhat, rstd, g_)
        dr_ref[...] = dr
        drb_ref[...] = (0.5 * dr).astype(BF16)

        @pl.when(i == 0)
        def _():
            gs_ref[...] = jnp.zeros_like(gs_ref)
            bs_ref[...] = jnp.zeros_like(bs_ref)
            ls_ref[...] = jnp.zeros_like(ls_ref)

        gs_ref[...] += jnp.sum(dy * xhat, axis=0, keepdims=True)
        bs_ref[...] += jnp.sum(dy, axis=0, keepdims=True)
        ls_ref[...] += jnp.sum(err * err, axis=0, keepdims=True)

    row = pl.BlockSpec((tm, D), lambda i: (i, 0))
    vec = pl.BlockSpec((1, D), lambda i: (0, 0))
    col = pl.BlockSpec((tm, 1), lambda i: (i, 0))
    return pl.pallas_call(
        body, name=name, grid=(T // tm,), in_specs=[row, row, row, col, vec, vec],
        out_specs=[row, row, vec, vec, vec],
        out_shape=[jax.ShapeDtypeStruct((T, D), F32), jax.ShapeDtypeStruct((T, D), BF16),
                   jax.ShapeDtypeStruct((1, D), F32), jax.ShapeDtypeStruct((1, D), F32),
                   jax.ShapeDtypeStruct((1, D), F32)],
        compiler_params=_params(),
    )(hprev, f, target, mask, g, b)


def _down(x, k):
    rows = lax.broadcasted_iota(jnp.int32, x.shape, 0)
    return jnp.where(rows >= k, pltpu.roll(x, k, axis=0), 0.0)


def _up(x, k):
    n = x.shape[0]
    rows = lax.broadcasted_iota(jnp.int32, x.shape, 0)
    return jnp.where(rows < n - k, pltpu.roll(x, n - k, axis=0), 0.0)


def _conv_fwd(u, conv_w, L, CC, name):
    T = u.shape[0]
    tc = _tile(CC, (256, 128))
    n = CC // tc

    def body(b_ref, c_ref, x_ref, w_ref, o_ref):
        cx = c_ref[...] * x_ref[...]
        w = w_ref[...]
        conv = w[0:1] * _down(cx, 2) + w[1:2] * _down(cx, 1) + w[2:3] * cx
        o_ref[...] = (b_ref[...] * conv).astype(BF16)

    blk = lambda off: pl.BlockSpec((L, tc), lambda s, j: (s, j + off * n))
    return pl.pallas_call(
        body, name=name, grid=(T // L, n),
        in_specs=[blk(0), blk(1), blk(2), pl.BlockSpec((3, tc), lambda s, j: (0, j))],
        out_specs=pl.BlockSpec((L, tc), lambda s, j: (s, j)),
        out_shape=jax.ShapeDtypeStruct((T, CC), BF16), compiler_params=_params(),
    )(u, u, u, conv_w)


def _conv_bwd(u, dy, conv_w, L, CC, name):
    T = u.shape[0]
    tc = _tile(CC, (256, 128))
    n = CC // tc

    def body(b_ref, c_ref, x_ref, dy_ref, w_ref, db_ref, dc_ref, dx_ref, dw_ref):
        s = pl.program_id(1)
        c_, x_ = c_ref[...], x_ref[...]
        cx = c_ * x_
        w = w_ref[...]
        cx1, cx2 = _down(cx, 1), _down(cx, 2)
        conv = w[0:1] * cx2 + w[1:2] * cx1 + w[2:3] * cx
        dy_ = dy_ref[...]
        db_ref[...] = (dy_ * conv).astype(BF16)
        dconv = dy_ * b_ref[...]
        dcx = w[2:3] * dconv + w[1:2] * _up(dconv, 1) + w[0:1] * _up(dconv, 2)
        dc_ref[...] = (dcx * x_).astype(BF16)
        dx_ref[...] = (dcx * c_).astype(BF16)

        @pl.when(s == 0)
        def _():
            dw_ref[...] = jnp.zeros_like(dw_ref)

        dw_ref[0:1, :] += jnp.sum(dconv * cx2, axis=0, keepdims=True)
        dw_ref[1:2, :] += jnp.sum(dconv * cx1, axis=0, keepdims=True)
        dw_ref[2:3, :] += jnp.sum(dconv * cx, axis=0, keepdims=True)

    blk = lambda off: pl.BlockSpec((L, tc), lambda j, s: (s, j + off * n))
    out = pl.BlockSpec((L, tc), lambda j, s: (s, j))
    act = jax.ShapeDtypeStruct((T, CC), BF16)
    return pl.pallas_call(
        body, name=name, grid=(n, T // L),
        in_specs=[blk(0), blk(1), blk(2), blk(0), pl.BlockSpec((3, tc), lambda j, s: (0, j))],
        out_specs=[out, out, out, pl.BlockSpec((3, tc), lambda j, s: (0, j))],
        out_shape=[act, act, act, jax.ShapeDtypeStruct((3, CC), F32)], compiler_params=_params(),
    )(u, u, u, dy, conv_w)


def _pool_counts(shape, g):
    rows = lax.broadcasted_iota(jnp.int32, shape, 0)
    win = jnp.left_shift(jnp.int32(POOL_WINDOWS[0]), g)
    return jnp.minimum(rows + 1, win).astype(F32)


def _pick_group(g, vals):
    out = vals[-1]
    for i in range(len(vals) - 2, -1, -1):
        out = jnp.where(g == i, vals[i], out)
    return out


def _pool_fwd(u, pool_w, pool_scale, L, CC, PG, name):
    T = u.shape[0]
    G = len(POOL_WINDOWS)
    off = 3 * CC // PG

    def body(z_ref, w_ref, sc_ref, y_ref, d_ref):
        g = pl.program_id(1)
        z = z_ref[...]
        s1 = z + _down(z, 1)
        s2 = s1 + _down(s1, 2)
        s3 = s2 + _down(s2, 4)
        s4 = s3 + _down(s3, 8)
        pooled = _pick_group(g, [s1, s2, s3, s4]) / _pool_counts(z.shape, g)
        d = (pooled - z).astype(BF16)
        d_ref[...] = d
        q = jnp.dot(d, w_ref[...], preferred_element_type=F32)
        y_ref[...] = (q * sc_ref[...]).astype(BF16)

    blk = pl.BlockSpec((L, PG), lambda s, g: (s, g))
    act = jax.ShapeDtypeStruct((T, G * PG), BF16)
    return pl.pallas_call(
        body, name=name, grid=(T // L, G),
        in_specs=[pl.BlockSpec((L, PG), lambda s, g: (s, off + g)), pl.BlockSpec((None, PG, PG), lambda s, g: (g, 0, 0)),
                  pl.BlockSpec((1, PG), lambda s, g: (0, g))],
        out_specs=[blk, blk], out_shape=[act, act], compiler_params=_params(),
    )(u, pool_w, pool_scale)


def _pool_bwd(d, dy, pool_w, pool_scale, L, CC, PG, name):
    T = d.shape[0]
    G = len(POOL_WINDOWS)
    off = CC // PG

    def body(d_ref, dy_ref, w_ref, sc_ref, dz_ref, dw_ref, dsc_ref):
        g = pl.program_id(0)
        s = pl.program_id(1)
        d_ = d_ref[...]
        w = w_ref[...]
        dy_ = dy_ref[...]
        q = jnp.dot(d_, w, preferred_element_type=F32)
        dq = (dy_ * sc_ref[...]).astype(BF16)
        dd = lax.dot_general(dq, w, (((1,), (1,)), ((), ())), preferred_element_type=F32)
        e = dd / _pool_counts(dd.shape, g)
        a1 = e + _up(e, 1)
        a2 = a1 + _up(a1, 2)
        a3 = a2 + _up(a2, 4)
        a4 = a3 + _up(a3, 8)
        dz_ref[...] = (_pick_group(g, [a1, a2, a3, a4]) - dd).astype(BF16)

        @pl.when(s == 0)
        def _():
            dw_ref[...] = jnp.zeros_like(dw_ref)
            dsc_ref[...] = jnp.zeros_like(dsc_ref)

        dw_ref[...] += lax.dot_general(d_, dq, (((0,), (0,)), ((), ())), preferred_element_type=F32)
        dsc_ref[...] += jnp.sum(dy_ * q, axis=0, keepdims=True)

    return pl.pallas_call(
        body, name=name, grid=(G, T // L),
        in_specs=[pl.BlockSpec((L, PG), lambda g, s: (s, g)), pl.BlockSpec((L, PG), lambda g, s: (s, off + g)),
                  pl.BlockSpec((None, PG, PG), lambda g, s: (g, 0, 0)), pl.BlockSpec((1, PG), lambda g, s: (0, g))],
        out_specs=[pl.BlockSpec((L, PG), lambda g, s: (s, g)), pl.BlockSpec((None, PG, PG), lambda g, s: (g, 0, 0)),
                   pl.BlockSpec((1, PG), lambda g, s: (0, g))],
        out_shape=[jax.ShapeDtypeStruct((T, G * PG), BF16), jax.ShapeDtypeStruct((G, PG, PG), F32),
                   jax.ShapeDtypeStruct((1, G * PG), F32)],
        compiler_params=_params(),
    )(d, dy, pool_w, pool_scale)


def _input_grad(dr1, dh_branch, L, name):
    T, D = dr1.shape
    nseq = T // L
    tc = _tile(D, (512, 256, 128))

    def body(a_ref, b_ref, gx_ref, gm_ref):
        dh = ALPHA * a_ref[...] + b_ref[...]
        gm_ref[...] = dh[:N_META]
        gx_ref[...] = dh[N_META:]

    blk = pl.BlockSpec((L, tc), lambda s, j: (s, j))
    return pl.pallas_call(
        body, name=name, grid=(nseq, D // tc), in_specs=[blk, blk],
        out_specs=[pl.BlockSpec((None, L - N_META, tc), lambda s, j: (s, 0, j)),
                   pl.BlockSpec((None, N_META, tc), lambda s, j: (s, 0, j))],
        out_shape=[jax.ShapeDtypeStruct((nseq, L - N_META, D), F32), jax.ShapeDtypeStruct((nseq, N_META, D), F32)],
        compiler_params=_params(),
    )(dr1, dh_branch)


def _add2_bf16(a, b, name):
    R, C = a.shape
    tr = _rows_tile(R, C)

    def body(a_ref, b_ref, o_ref):
        o_ref[...] = (a_ref[...].astype(F32) + b_ref[...].astype(F32)).astype(BF16)

    blk = pl.BlockSpec((tr, C), lambda i: (i, 0))
    return pl.pallas_call(body, name=name, grid=(R // tr,), in_specs=[blk, blk], out_specs=blk,
                          out_shape=jax.ShapeDtypeStruct((R, C), BF16), compiler_params=_params())(a, b)


def _sum4_f32(own, recv, name):
    R, C = own.shape
    tr = _rows_tile(R, C)

    def body(o_ref, r_ref, out_ref):
        acc = o_ref[...].astype(F32)
        for k in range(3):
            acc = acc + r_ref[k].astype(F32)
        out_ref[...] = acc

    blk = pl.BlockSpec((tr, C), lambda i: (i, 0))
    return pl.pallas_call(body, name=name, grid=(R // tr,),
                          in_specs=[blk, pl.BlockSpec((3, tr, C), lambda i: (0, i, 0))], out_specs=blk,
                          out_shape=jax.ShapeDtypeStruct((R, C), F32), compiler_params=_params())(own, recv)


def _adamw(w, g, m, v, name):
    R, C = w.shape
    tr = _rows_tile(R, C, mult=8)

    def body(w_ref, g_ref, m_ref, v_ref, d_ref, nm_ref, nv_ref):
        g_ = g_ref[...]
        m_ = ADAM_B1 * m_ref[...] + (1.0 - ADAM_B1) * g_
        v_ = ADAM_B2 * v_ref[...] + (1.0 - ADAM_B2) * (g_ * g_)
        m_hat = m_ / (1.0 - ADAM_B1 ** ADAM_STEP)
        v_hat = v_ / (1.0 - ADAM_B2 ** ADAM_STEP)
        d_ref[...] = -ADAM_LR * (m_hat / (jnp.sqrt(v_hat) + ADAM_EPS) + ADAM_WD * w_ref[...])
        nm_ref[...] = m_
        nv_ref[...] = v_

    blk = pl.BlockSpec((tr, C), lambda i: (i, 0))
    shp = jax.ShapeDtypeStruct((R, C), F32)
    return pl.pallas_call(body, name=name, grid=(R // tr,), in_specs=[blk] * 4, out_specs=[blk] * 3,
                          out_shape=[shp] * 3, compiler_params=_params())(w, g, m, v)


def _pos():
    return lax.axis_index("x"), lax.axis_index("y"), lax.axis_index("c")


def _other_chips(x, y):
    return [(1 - x, y), (x, 1 - y), (1 - x, 1 - y)]


class _Sharded:
    def __init__(self, full_shape, axis):
        self.rows, self.cols = full_shape
        self.axis = axis
        self.srows = self.rows // 4 if axis == 0 else self.rows
        self.scols = self.cols if axis == 0 else self.cols // 4
        assert self.srows % 2 == 0
        self.h = self.srows // 2
        self.half_all_shape = (4 * self.h, self.cols) if axis == 0 else (self.h, self.cols)
        self.half_shape = (self.h, self.scols)

    def shard(self, ref, q):
        if self.axis == 0:
            return ref.at[pl.ds(q * self.srows, self.srows), :]
        return ref.at[:, pl.ds(pl.multiple_of(q * self.scols, LANES), self.scols)]

    def half(self, ref, q, c):
        if self.axis == 0:
            return ref.at[pl.ds(q * self.srows + c * self.h, self.h), :]
        return ref.at[pl.ds(c * self.h, self.h), pl.ds(pl.multiple_of(q * self.scols, LANES), self.scols)]

    def half_in_stack(self, ref, q):
        if self.axis == 0:
            return ref.at[pl.ds(q * self.h, self.h), :]
        return ref.at[:, pl.ds(pl.multiple_of(q * self.scols, LANES), self.scols)]


ANY = pl.BlockSpec(memory_space=pl.ANY)


def _gather_weight(wb, sh, name):
    def body(w_ref, out_ref, send_sems, recv_sems, local_sem):
        x, y, c = _pos()
        q = 2 * x + y
        sib = (x, y, 1 - c)
        chips = _other_chips(x, y)

        def rcopy(src, dst, k, to):
            return pltpu.make_async_remote_copy(src_ref=src, dst_ref=dst, send_sem=send_sems.at[k],
                                                recv_sem=recv_sems.at[k], device_id=to, device_id_type=MESH)

        mine = pltpu.make_async_copy(w_ref, sh.shard(out_ref, q), local_sem)
        mine.start()
        src = w_ref.at[pl.ds(c * sh.h, sh.h), :]
        first = [rcopy(src, sh.half(out_ref, q, c), j, (cx, cy, c)) for j, (cx, cy) in enumerate(chips)]
        for cp in first:
            cp.start()
        passed = []
        for j, (cx, cy) in enumerate(chips):
            landed = sh.half(out_ref, 2 * cx + cy, c)
            rcopy(landed, landed, j, sib).wait_recv()
            fwd = rcopy(landed, landed, 3 + j, sib)
            fwd.start()
            passed.append(fwd)
        for j, (cx, cy) in enumerate(chips):
            theirs = sh.half(out_ref, 2 * cx + cy, 1 - c)
            rcopy(theirs, theirs, 3 + j, sib).wait_recv()
        for cp in first + passed:
            cp.wait_send()
        mine.wait()

    return pl.pallas_call(
        body, name=name, in_specs=[ANY], out_specs=ANY,
        out_shape=jax.ShapeDtypeStruct((sh.rows, sh.cols), wb.dtype),
        scratch_shapes=[pltpu.SemaphoreType.DMA((6,)), pltpu.SemaphoreType.DMA((6,)), pltpu.SemaphoreType.DMA],
    )(wb)


def _rs_sibling(dw, sh, name):
    n = 4 if sh.axis == 0 else 1

    def body(dw_ref, mine_ref, theirs_ref, send_sems, recv_sems, local_sems):
        x, y, c = _pos()
        sib = (x, y, 1 - c)
        locals_, cps = [], []
        for k in range(n):
            if sh.axis == 0:
                keep = dw_ref.at[pl.ds(k * sh.srows + c * sh.h, sh.h), :]
                give = dw_ref.at[pl.ds(k * sh.srows + (1 - c) * sh.h, sh.h), :]
                mine_k = mine_ref.at[pl.ds(k * sh.h, sh.h), :]
                theirs_k = theirs_ref.at[pl.ds(k * sh.h, sh.h), :]
            else:
                keep = dw_ref.at[pl.ds(c * sh.h, sh.h), :]
                give = dw_ref.at[pl.ds((1 - c) * sh.h, sh.h), :]
                mine_k, theirs_k = mine_ref, theirs_ref
            locals_.append(pltpu.make_async_copy(keep, mine_k, local_sems.at[k]))
            cps.append(pltpu.make_async_remote_copy(src_ref=give, dst_ref=theirs_k, send_sem=send_sems.at[k],
                                                    recv_sem=recv_sems.at[k], device_id=sib, device_id_type=MESH))
        for cp in locals_ + cps:
            cp.start()
        for cp in cps + locals_:
            cp.wait()

    shp = jax.ShapeDtypeStruct(sh.half_all_shape, dw.dtype)
    return pl.pallas_call(
        body, name=name, in_specs=[ANY], out_specs=[ANY, ANY], out_shape=[shp, shp],
        scratch_shapes=[pltpu.SemaphoreType.DMA((n,)), pltpu.SemaphoreType.DMA((n,)), pltpu.SemaphoreType.DMA((n,))],
    )(dw)


def _rs_chips(p, sh, name):
    def body(p_ref, own_ref, recv_ref, send_sems, recv_sems, local_sem):
        x, y, c = _pos()
        q = 2 * x + y
        chips = _other_chips(x, y)
        local = pltpu.make_async_copy(sh.half_in_stack(p_ref, q), own_ref, local_sem)
        local.start()
        cps = [pltpu.make_async_remote_copy(src_ref=sh.half_in_stack(p_ref, 2 * cx + cy), dst_ref=recv_ref.at[j],
                                            send_sem=send_sems.at[j], recv_sem=recv_sems.at[j],
                                            device_id=(cx, cy, c), device_id_type=MESH)
               for j, (cx, cy) in enumerate(chips)]
        for cp in cps:
            cp.start()
        for cp in cps:
            cp.wait()
        local.wait()

    return pl.pallas_call(
        body, name=name, in_specs=[ANY], out_specs=[ANY, ANY],
        out_shape=[jax.ShapeDtypeStruct(sh.half_shape, p.dtype), jax.ShapeDtypeStruct((3,) + sh.half_shape, p.dtype)],
        scratch_shapes=[pltpu.SemaphoreType.DMA((3,)), pltpu.SemaphoreType.DMA((3,)), pltpu.SemaphoreType.DMA],
    )(p)


def _share_halves(g_half, sh, name):
    def body(g_ref, out_ref, send_sem, recv_sem, local_sem):
        x, y, c = _pos()
        mine = out_ref.at[pl.ds(c * sh.h, sh.h), :]
        local = pltpu.make_async_copy(g_ref, mine, local_sem)
        local.start()
        cp = pltpu.make_async_remote_copy(src_ref=g_ref, dst_ref=mine, send_sem=send_sem, recv_sem=recv_sem,
                                          device_id=(x, y, 1 - c), device_id_type=MESH)
        cp.start()
        cp.wait()
        local.wait()

    return pl.pallas_call(
        body, name=name, in_specs=[ANY], out_specs=ANY,
        out_shape=jax.ShapeDtypeStruct((sh.srows, sh.scols), g_half.dtype),
        scratch_shapes=[pltpu.SemaphoreType.DMA, pltpu.SemaphoreType.DMA, pltpu.SemaphoreType.DMA],
    )(g_half)


def _exchange_small(v, reduce, name):
    R, C = v.shape

    def body(v_ref, o_ref, buf, send_sems, recv_sems):
        x, y, c = _pos()
        me = 4 * x + 2 * y + c
        slots = buf if reduce else o_ref
        slots[me] = v_ref[...]
        cps = []
        for k in range(1, 8):
            px = 1 - x if k & 4 else x
            py = 1 - y if k & 2 else y
            pc = 1 - c if k & 1 else c
            cps.append((pltpu.make_async_remote_copy(
                src_ref=v_ref, dst_ref=slots.at[me], send_sem=send_sems.at[k - 1], recv_sem=recv_sems.at[k - 1],
                device_id=(px, py, pc), device_id_type=MESH), 4 * px + 2 * py + pc))
        for cp, _ in cps:
            cp.start()
        for k, (cp, peer) in enumerate(cps):
            pltpu.make_async_remote_copy(
                src_ref=v_ref, dst_ref=slots.at[peer], send_sem=send_sems.at[k], recv_sem=recv_sems.at[k],
                device_id=(x, y, c), device_id_type=MESH).wait_recv()
        if reduce:
            acc = buf[0]
            for d in range(1, 8):
                acc = acc + buf[d]
            o_ref[...] = acc
        for cp, _ in cps:
            cp.wait_send()

    vm = pl.BlockSpec(memory_space=pltpu.VMEM)
    out_shape = jax.ShapeDtypeStruct((R, C) if reduce else (8, R, C), v.dtype)
    scratch = [pltpu.VMEM((8, R, C) if reduce else (8, LANES), v.dtype),
               pltpu.SemaphoreType.DMA((7,)), pltpu.SemaphoreType.DMA((7,))]
    return pl.pallas_call(body, name=name, in_specs=[vm], out_specs=vm, out_shape=out_shape, scratch_shapes=scratch,
                          compiler_params=_params())(v)


def _reduce_scatter_weight(dw, sh, tag):
    mine, theirs = _rs_sibling(dw, sh, "rs_sib_" + tag)
    part = _add2_bf16(mine, theirs, "rs_add_" + tag)
    own, recv = _rs_chips(part, sh, "rs_ici_" + tag)
    g_half = _sum4_f32(own, recv, "rs_sum_" + tag)
    return _share_halves(g_half, sh, "rs_share_" + tag)


def _pack_rows(parts, width):
    rows, offs, at = [], [], 0
    for p in parts:
        flat = p.reshape(-1)
        n = 8 * -(-flat.shape[0] // (8 * width))
        flat = jnp.pad(flat, (0, n * width - flat.shape[0]))
        rows.append(flat.reshape(n, width))
        offs.append(at)
        at += n
    return jnp.concatenate(rows, axis=0), offs


def _unpack_rows(packed, offs, shapes):
    out = []
    for off, shp in zip(offs, shapes):
        size = 1
        for s in shp:
            size *= s
        n = -(-size // packed.shape[1])
        out.append(packed[off:off + n].reshape(-1)[:size].reshape(shp))
    return out


def kernel(x, meta_tokens, ffn1_w_gu, ffn1_w_down, ln1_g, ln1_b, w_in, conv_w, pool_w, pool_scale, w_out, ln2_g, ln2_b, ffn2_w_gu, ffn2_w_down, ln3_g, ln3_b, loss_target, m_meta_tokens, m_ffn1_w_gu, m_ffn1_w_down, m_ln1_g, m_ln1_b, m_w_in, m_conv_w, m_pool_w, m_pool_scale, m_w_out, m_ln2_g, m_ln2_b, m_ffn2_w_gu, m_ffn2_w_down, m_ln3_g, m_ln3_b, v_meta_tokens, v_ffn1_w_gu, v_ffn1_w_down, v_ln1_g, v_ln1_b, v_w_in, v_conv_w, v_pool_w, v_pool_scale, v_w_out, v_ln2_g, v_ln2_b, v_ffn2_w_gu, v_ffn2_w_down, v_ln3_g, v_ln3_b):
    nseq, S, D = x.shape
    L = S + N_META
    T = nseq * L
    F = ffn1_w_down.shape[1] * 4
    CC = conv_w.shape[2] * 4
    G, PGs, PG = pool_w.shape[1:]
    PC = G * PG
    IN = w_in.shape[2] * 4
    qx, qy, qc = _pos()
    q = 2 * qx + qy

    sh_gu = _Sharded((D, 2 * F), 1)
    sh_down = _Sharded((F, D), 0)
    sh_in = _Sharded((D, IN), 1)
    sh_out = _Sharded((CC + PC, D), 0)

    wgu1 = _gather_weight(ffn1_w_gu[0].astype(BF16), sh_gu, "ag_gu")
    wd1 = _gather_weight(ffn1_w_down[0].astype(BF16), sh_down, "ag_down")
    win = _gather_weight(w_in[0].astype(BF16), sh_in, "ag_in")
    wout = _gather_weight(w_out[0].astype(BF16), sh_out, "ag_out")
    wgu2 = _gather_weight(ffn2_w_gu[0].astype(BF16), sh_gu, "ag_gu")
    wd2 = _gather_weight(ffn2_w_down[0].astype(BF16), sh_down, "ag_down")

    small_w = [meta_tokens, conv_w[0], pool_w[0]]
    packed, offs = _pack_rows(small_w, 4 * LANES)
    slots = _exchange_small(packed, False, "ag_small")
    per_chip = [_unpack_rows(slots[2 * k], offs, [p.shape for p in small_w]) for k in range(4)]
    meta_full = jnp.concatenate([p[0] for p in per_chip], axis=1)
    conv_full = jnp.concatenate([p[1] for p in per_chip], axis=1)
    poolw_full = jnp.concatenate([p[2] for p in per_chip], axis=1).astype(BF16)
    pscale = pool_scale

    h0 = jnp.concatenate([jnp.broadcast_to(meta_full[None], (nseq, N_META, D)), x], axis=1).reshape(T, D)
    h0b = h0.astype(BF16)
    tgt = jnp.pad(loss_target, ((0, 0), (N_META, 0), (0, 0))).reshape(T, D)
    mask = (lax.broadcasted_iota(jnp.int32, (nseq, L, 1), 1) >= N_META).astype(F32).reshape(T, 1)

    gu1 = _mm_nn(h0b, wgu1, BF16, "ffn_gu")
    a1 = _silu_mul(gu1, "silu_mul")
    f1 = _mm_nn(a1, wd1, F32, "ffn_down")
    h1, h1b, xh1, rs1 = _ln_fwd(h0, f1, ln1_g, ln1_b, 0.5, "ln_fwd")

    u = _mm_nn(h1b, win, F32, "mix_in")
    yc = _conv_fwd(u, conv_full, L, CC, "conv_fwd")
    yp, dpool = _pool_fwd(u, poolw_full, pscale, L, CC, PG, "pool_fwd")
    ymix = jnp.concatenate([yc, yp], axis=1)
    o = _mm_nn(ymix, wout, F32, "mix_out")
    h2, h2b, xh2, rs2 = _ln_fwd(h1, o, ln2_g, ln2_b, 1.0, "ln_fwd")

    gu2 = _mm_nn(h2b, wgu2, BF16, "ffn_gu")
    a2 = _silu_mul(gu2, "silu_mul")
    f2 = _mm_nn(a2, wd2, F32, "ffn_down")
    dr3, dr3b, gs3, bs3, lsum = _ln3_loss(h2, f2, tgt, mask, ln3_g, ln3_b, "ln3_loss")
    loss = lax.psum(0.5 * jnp.sum(lsum) / D, ALL_AXES)

    def ffn_bwd(drb, a, gu, hb, wgu, wd):
        da = _mm_nt(drb, wd, BF16, "ffn_da")
        dwd = _mm_tn(a, drb, "ffn_dwd")
        dgu = _swiglu_bwd(da, gu, "swiglu_bwd")
        dh = _mm_nt(dgu, wgu, F32, "ffn_dh")
        dwgu = _mm_tn(hb, dgu, "ffn_dwgu")
        return dh, dwgu, dwd

    dh2, dwgu2, dwd2 = ffn_bwd(dr3b, a2, gu2, h2b, wgu2, wd2)
    g_gu2 = _reduce_scatter_weight(dwgu2, sh_gu, "gu")
    g_d2 = _reduce_scatter_weight(dwd2, sh_down, "down")

    dr2, dr2b, gs2, bs2 = _ln_bwd(dr3, dh2, xh2, rs2, ln2_g, 1.0, "ln_bwd")
    dymix = _mm_nt(dr2b, wout, F32, "mix_dy")
    dwout = _mm_tn(ymix, dr2b, "mix_dwout")
    db_, dc_, dx_, dconvw = _conv_bwd(u, dymix, conv_full, L, CC, "conv_bwd")
    dz_, dpoolw, dpscale = _pool_bwd(dpool, dymix, poolw_full, pscale, L, CC, PG, "pool_bwd")
    du = jnp.concatenate([db_, dc_, dx_, dz_], axis=1)
    dh1 = _mm_nt(du, win, F32, "mix_dh")
    dwin = _mm_tn(h1b, du, "mix_dwin")
    g_out = _reduce_scatter_weight(dwout, sh_out, "out")
    g_in = _reduce_scatter_weight(dwin, sh_in, "in")

    dr1, dr1b, gs1, bs1 = _ln_bwd(dr2, dh1, xh1, rs1, ln1_g, 0.5, "ln_bwd")
    dh0f, dwgu1, dwd1 = ffn_bwd(dr1b, a1, gu1, h0b, wgu1, wd1)
    g_gu1 = _reduce_scatter_weight(dwgu1, sh_gu, "gu")
    g_d1 = _reduce_scatter_weight(dwd1, sh_down, "down")
    grad_x, gmeta = _input_grad(dr1, dh0f, L, "input_grad")

    small_g = [gs1, bs1, gs2, bs2, gs3, bs3, dpscale, dconvw, jnp.sum(gmeta, axis=0)]
    gpacked, goffs = _pack_rows(small_g, D)
    gsum = _exchange_small(gpacked, True, "ar_small")
    (g_ln1g, g_ln1b, g_ln2g, g_ln2b, g_ln3g, g_ln3b, g_pscale, g_conv_all, g_meta_all) = _unpack_rows(
        gsum, goffs, [p.shape for p in small_g])
    g_pool_all = _exchange_small(dpoolw.reshape(-1, PG), True, "ar_pool").reshape(G, PG, PG)
    g_meta = lax.dynamic_slice_in_dim(g_meta_all, q * (D // 4), D // 4, axis=1)
    g_conv = lax.dynamic_slice_in_dim(g_conv_all, q * (CC // 4), CC // 4, axis=1)
    g_pool = lax.dynamic_slice_in_dim(g_pool_all, q * PGs, PGs, axis=1)

    names = ["meta_tokens", "ffn1_w_gu", "ffn1_w_down", "ln1_g", "ln1_b", "w_in", "conv_w", "pool_w", "pool_scale",
             "w_out", "ln2_g", "ln2_b", "ffn2_w_gu", "ffn2_w_down", "ln3_g", "ln3_b"]
    ws = dict(zip(names, [meta_tokens, ffn1_w_gu, ffn1_w_down, ln1_g, ln1_b, w_in, conv_w, pool_w, pool_scale, w_out,
                          ln2_g, ln2_b, ffn2_w_gu, ffn2_w_down, ln3_g, ln3_b]))
    ms = dict(zip(names, [m_meta_tokens, m_ffn1_w_gu, m_ffn1_w_down, m_ln1_g, m_ln1_b, m_w_in, m_conv_w, m_pool_w,
                          m_pool_scale, m_w_out, m_ln2_g, m_ln2_b, m_ffn2_w_gu, m_ffn2_w_down, m_ln3_g, m_ln3_b]))
    vs = dict(zip(names, [v_meta_tokens, v_ffn1_w_gu, v_ffn1_w_down, v_ln1_g, v_ln1_b, v_w_in, v_conv_w, v_pool_w,
                          v_pool_scale, v_w_out, v_ln2_g, v_ln2_b, v_ffn2_w_gu, v_ffn2_w_down, v_ln3_g, v_ln3_b]))
    grads = {
        "meta_tokens": g_meta, "ffn1_w_gu": g_gu1[None], "ffn1_w_down": g_d1[None], "ln1_g": g_ln1g, "ln1_b": g_ln1b,
        "w_in": g_in[None], "conv_w": g_conv[None], "pool_w": g_pool[None], "pool_scale": g_pscale,
        "w_out": g_out[None], "ln2_g": g_ln2g, "ln2_b": g_ln2b, "ffn2_w_gu": g_gu2[None], "ffn2_w_down": g_d2[None],
        "ln3_g": g_ln3g, "ln3_b": g_ln3b,
    }
    big = ("ffn1_w_gu", "ffn1_w_down", "w_in", "w_out", "ffn2_w_gu", "ffn2_w_down")
    delta, new_m, new_v = {}, {}, {}
    for n in big:
        d_, m_, v_ = _adamw(ws[n][0], grads[n][0], ms[n][0], vs[n][0], "adamw_" + n.split("_", 1)[1])
        delta[n], new_m[n], new_v[n] = d_[None], m_[None], v_[None]
    small = [n for n in names if n not in big]
    width = 4 * LANES
    pw, poffs = _pack_rows([ws[n] for n in small], width)
    pg, _ = _pack_rows([grads[n] for n in small], width)
    pm, _ = _pack_rows([ms[n] for n in small], width)
    pv, _ = _pack_rows([vs[n] for n in small], width)
    sd, sm, sv = _adamw(pw, pg, pm, pv, "adamw_small")
    shapes = [ws[n].shape for n in small]
    for n, d_, m_, v_ in zip(small, _unpack_rows(sd, poffs, shapes), _unpack_rows(sm, poffs, shapes),
                             _unpack_rows(sv, poffs, shapes)):
        delta[n], new_m[n], new_v[n] = d_, m_, v_

    return (loss, grad_x, *[grads[n] for n in names], *[delta[n] for n in names], *[new_m[n] for n in names],
            *[new_v[n] for n in names])
```

```python
import jax
import jax.numpy as jnp
from jax import lax
from jax.experimental import pallas as pl
from jax.experimental.pallas import tpu as pltpu

F32 = jnp.float32
BF16 = jnp.bfloat16
MESH = pl.DeviceIdType.MESH
ALL_AXES = ("x", "y", "c")

N_META = 16
POOL_WINDOWS = (2, 4, 8, 16)
LN_EPS = 1e-5
ALPHA = 2.0 ** 0.25
ADAM_LR, ADAM_B1, ADAM_B2, ADAM_EPS, ADAM_WD, ADAM_STEP = 0.001, 0.9, 0.999, 1e-08, 0.01, 10

VMEM_LIMIT_V7X = 60 * 1024 * 1024
LANES = 128
ELEMENTWISE_BLOCK_BYTES = 2 * 1024 * 1024


def _params(**kw):
    return pltpu.CompilerParams(vmem_limit_bytes=VMEM_LIMIT_V7X, **kw)


def _tile(n, prefs):
    for t in prefs:
        if t <= n and n % t == 0:
            return t
    return n


def _rows_tile(rows, cols, mult=16):
    cap = max(mult, ELEMENTWISE_BLOCK_BYTES // (4 * cols))
    best = None
    for t in range(mult, min(rows, cap) + 1, mult):
        if rows % t == 0:
            best = t
    return best if best is not None else rows


def _matmul(a, b, *, mode, tm, tn, tk, out_dtype, name, a_outer=True, scale=1.0, after=None):
    if mode == "tn":
        K, M = a.shape
    else:
        M, K = a.shape
    N = b.shape[0] if mode == "nt" else b.shape[1]
    assert M % tm == 0 and N % tn == 0 and K % tk == 0, (name, M, N, K, tm, tn, tk)
    nk = K // tk
    if a_outer:
        grid = (M // tm, N // tn, nk)
        ij = lambda p, q: (p, q)
    else:
        grid = (N // tn, M // tm, nk)
        ij = lambda p, q: (q, p)

    if mode == "tn":
        a_spec = pl.BlockSpec((tk, tm), lambda p, q, k: (k, ij(p, q)[0]))
        dims = (((0,), (0,)), ((), ()))
    else:
        a_spec = pl.BlockSpec((tm, tk), lambda p, q, k: (ij(p, q)[0], k))
        dims = (((1,), (1,)), ((), ())) if mode == "nt" else (((1,), (0,)), ((), ()))
    if mode == "nt":
        b_spec = pl.BlockSpec((tn, tk), lambda p, q, k: (ij(p, q)[1], k))
    else:
        b_spec = pl.BlockSpec((tk, tn), lambda p, q, k: (k, ij(p, q)[1]))
    o_spec = pl.BlockSpec((tm, tn), lambda p, q, k: ij(p, q))

    def fin(acc):
        if scale != 1.0:
            acc = acc * scale
        return acc.astype(out_dtype)

    n_in = 2 if after is None else 3

    if nk == 1:
        def body(*refs):
            a_ref, b_ref, o_ref = refs[0], refs[1], refs[n_in]
            o_ref[...] = fin(lax.dot_general(a_ref[...], b_ref[...], dims, preferred_element_type=F32))
        scratch = []
    else:
        def body(*refs):
            a_ref, b_ref, o_ref, acc_ref = refs[0], refs[1], refs[n_in], refs[n_in + 1]
            k = pl.program_id(2)
            prod = lax.dot_general(a_ref[...], b_ref[...], dims, preferred_element_type=F32)

            @pl.when(k == 0)
            def _():
                acc_ref[...] = prod

            @pl.when(k > 0)
            def _():
                acc_ref[...] += prod

            @pl.when(k == nk - 1)
            def _():
                o_ref[...] = fin(acc_ref[...])
        scratch = [pltpu.VMEM((tm, tn), F32)]

    in_specs = [a_spec, b_spec] + ([] if after is None else [pl.BlockSpec(memory_space=pl.ANY)])
    return pl.pallas_call(
        body, name=name, grid=grid, in_specs=in_specs, out_specs=o_spec,
        out_shape=jax.ShapeDtypeStruct((M, N), out_dtype), scratch_shapes=scratch,
        compiler_params=_params(),
    )(a, b, *([] if after is None else [after]))


TM_PREFS = (1376, 688, 512, 256, 128, 64)


def _mm_nn(a, b, out_dtype, name, after=None):
    M, K = a.shape
    N = b.shape[1]
    big_k = K * 2 * 1376 > 12 * 1024 * 1024
    tm = _tile(M, (688,) + TM_PREFS[2:]) if big_k else _tile(M, TM_PREFS)
    tn = _tile(N, (256, 128)) if big_k else _tile(N, (512, 256, 128))
    return _matmul(a, b, mode="nn", tm=tm, tn=tn, tk=K, out_dtype=out_dtype, name=name, after=after)


def _mm_nt(a, b, out_dtype, name, scale=1.0, after=None):
    M, K = a.shape
    N = b.shape[0]
    tm = _tile(M, TM_PREFS)
    if K * 2 * tm > 12 * 1024 * 1024:
        tk = _tile(K, (2048, 512, 256, 128))
        tn = _tile(N, (2048,) if tk <= 512 else (1024, 512, 256, 128))
    else:
        tk = K
        tn = _tile(N, (512, 256, 128))
    return _matmul(a, b, mode="nt", tm=tm, tn=tn, tk=tk, out_dtype=out_dtype, name=name, scale=scale, after=after)


def _mm_tn(a, b, name, after=None):
    T, M = a.shape
    N = b.shape[1]
    if M % 2048 == 0:
        tm, tn, a_outer = 2048, _tile(N, (512, 256, 128)), True
    else:
        tm, tn, a_outer = _tile(M, (256, 128)), _tile(N, (2048, 1024, 512, 256, 128)), False
    return _matmul(a, b, mode="tn", tm=tm, tn=tn, tk=T, out_dtype=BF16, name=name, a_outer=a_outer, after=after)


def _silu_mul(gu, name):
    T, F2 = gu.shape
    F = F2 // 2
    tm = _rows_tile(T, F)

    def body(g_ref, u_ref, o_ref):
        g = g_ref[...].astype(F32)
        o_ref[...] = (g * jax.nn.sigmoid(g) * u_ref[...].astype(F32)).astype(BF16)

    return pl.pallas_call(
        body, name=name, grid=(T // tm,),
        in_specs=[pl.BlockSpec((tm, F), lambda i: (i, 0)), pl.BlockSpec((tm, F), lambda i: (i, 1))],
        out_specs=pl.BlockSpec((tm, F), lambda i: (i, 0)),
        out_shape=jax.ShapeDtypeStruct((T, F), BF16), compiler_params=_params(),
    )(gu, gu)


def _swiglu_bwd(da, gu, name):
    T, F2 = gu.shape
    F = F2 // 2
    tm = _rows_tile(T, F)

    def body(da_ref, g_ref, u_ref, o_ref):
        half = pl.program_id(1)
        g = g_ref[...].astype(F32)
        da_ = da_ref[...].astype(F32)
        s = jax.nn.sigmoid(g)

        @pl.when(half == 0)
        def _():
            o_ref[...] = (da_ * u_ref[...].astype(F32) * (s * (1.0 + g * (1.0 - s)))).astype(BF16)

        @pl.when(half == 1)
        def _():
            o_ref[...] = (da_ * (g * s)).astype(BF16)

    return pl.pallas_call(
        body, name=name, grid=(T // tm, 2),
        in_specs=[pl.BlockSpec((tm, F), lambda i, h: (i, 0)), pl.BlockSpec((tm, F), lambda i, h: (i, 0)),
                  pl.BlockSpec((tm, F), lambda i, h: (i, 1))],
        out_specs=pl.BlockSpec((tm, F), lambda i, h: (i, h)),
        out_shape=jax.ShapeDtypeStruct((T, F2), BF16), compiler_params=_params(),
    )(da, gu, gu)


def _ln_stats(r):
    mu = jnp.mean(r, axis=-1, keepdims=True)
    xc = r - mu
    var = jnp.mean(xc * xc, axis=-1, keepdims=True)
    rstd = lax.rsqrt(var + LN_EPS)
    return xc * rstd, rstd


def _ln_bwd_math(dh, xhat, rstd, g):
    dxh = dh * g
    m1 = jnp.mean(dxh, axis=-1, keepdims=True)
    m2 = jnp.mean(dxh * xhat, axis=-1, keepdims=True)
    return rstd * (dxh - m1 - xhat * m2)


def _ln_fwd(hprev, f, g, b, fscale, name):
    T, D = hprev.shape
    tm = _rows_tile(T, D)

    def body(hp_ref, f_ref, g_ref, b_ref, h_ref, hb_ref, xh_ref, rs_ref):
        r = ALPHA * hp_ref[...] + fscale * f_ref[...]
        xhat, rstd = _ln_stats(r)
        h = xhat * g_ref[...] + b_ref[...]
        h_ref[...] = h
        hb_ref[...] = h.astype(BF16)
        xh_ref[...] = xhat
        rs_ref[...] = rstd

    row = pl.BlockSpec((tm, D), lambda i: (i, 0))
    vec = pl.BlockSpec((1, D), lambda i: (0, 0))
    col = pl.BlockSpec((tm, 1), lambda i: (i, 0))
    return pl.pallas_call(
        body, name=name, grid=(T // tm,), in_specs=[row, row, vec, vec], out_specs=[row, row, row, col],
        out_shape=[jax.ShapeDtypeStruct((T, D), F32), jax.ShapeDtypeStruct((T, D), BF16),
                   jax.ShapeDtypeStruct((T, D), F32), jax.ShapeDtypeStruct((T, 1), F32)],
        compiler_params=_params(),
    )(hprev, f, g, b)


def _ln_bwd(dr_next, dh_branch, xhat, rstd, g, fscale, name):
    T, D = xhat.shape
    tm = _rows_tile(T, D)

    def body(dn_ref, db_ref, xh_ref, rs_ref, g_ref, dr_ref, drb_ref, gs_ref, bs_ref):
        i = pl.program_id(0)
        dh = ALPHA * dn_ref[...] + db_ref[...]
        xhat_ = xh_ref[...]
        dr = _ln_bwd_math(dh, xhat_, rs_ref[...], g_ref[...])
        dr_ref[...] = dr
        drb_ref[...] = (fscale * dr).astype(BF16)

        @pl.when(i == 0)
        def _():
            gs_ref[...] = jnp.zeros_like(gs_ref)
            bs_ref[...] = jnp.zeros_like(bs_ref)

        gs_ref[...] += jnp.sum(dh * xhat_, axis=0, keepdims=True)
        bs_ref[...] += jnp.sum(dh, axis=0, keepdims=True)

    row = pl.BlockSpec((tm, D), lambda i: (i, 0))
    vec = pl.BlockSpec((1, D), lambda i: (0, 0))
    col = pl.BlockSpec((tm, 1), lambda i: (i, 0))
    return pl.pallas_call(
        body, name=name, grid=(T // tm,), in_specs=[row, row, row, col, vec], out_specs=[row, row, vec, vec],
        out_shape=[jax.ShapeDtypeStruct((T, D), F32), jax.ShapeDtypeStruct((T, D), BF16),
                   jax.ShapeDtypeStruct((1, D), F32), jax.ShapeDtypeStruct((1, D), F32)],
        compiler_params=_params(),
    )(dr_next, dh_branch, xhat, rstd, g)


def _ln3_loss(hprev, f, target, mask, g, b, name):
    T, D = hprev.shape
    tm = _rows_tile(T, D)

    def body(hp_ref, f_ref, t_ref, m_ref, g_ref, b_ref, dr_ref, drb_ref, gs_ref, bs_ref, ls_ref):
        i = pl.program_id(0)
        r = ALPHA * hp_ref[...] + 0.5 * f_ref[...]
        xhat, rstd = _ln_stats(r)
        g_ = g_ref[...]
        y = xhat * g_ + b_ref[...]
        err = (y - t_ref[...]) * m_ref[...]
        dy = err * (1.0 / D)
        dr = _ln_bwd_math(dy, xhat, rstd, g_)
        dr_ref[...] = dr
        drb_ref[...] = (0.5 * dr).astype(BF16)

        @pl.when(i == 0)
        def _():
            gs_ref[...] = jnp.zeros_like(gs_ref)
            bs_ref[...] = jnp.zeros_like(bs_ref)
            ls_ref[...] = jnp.zeros_like(ls_ref)

        gs_ref[...] += jnp.sum(dy * xhat, axis=0, keepdims=True)
        bs_ref[...] += jnp.sum(dy, axis=0, keepdims=True)
        ls_ref[...] += jnp.sum(err * err, axis=0, keepdims=True)

    row = pl.BlockSpec((tm, D), lambda i: (i, 0))
    vec = pl.BlockSpec((1, D), lambda i: (0, 0))
    col = pl.BlockSpec((tm, 1), lambda i: (i, 0))
    return pl.pallas_call(
        body, name=name, grid=(T // tm,), in_specs=[row, row, row, col, vec, vec],
        out_specs=[row, row, vec, vec, vec],
        out_shape=[jax.ShapeDtypeStruct((T, D), F32), jax.ShapeDtypeStruct((T, D), BF16),
                   jax.ShapeDtypeStruct((1, D), F32), jax.ShapeDtypeStruct((1, D), F32),
                   jax.ShapeDtypeStruct((1, D), F32)],
        compiler_params=_params(),
    )(hprev, f, target, mask, g, b)


def _down(x, k):
    rows = lax.broadcasted_iota(jnp.int32, x.shape, 0)
    return jnp.where(rows >= k, pltpu.roll(x, k, axis=0), 0.0)


def _up(x, k):
    n = x.shape[0]
    rows = lax.broadcasted_iota(jnp.int32, x.shape, 0)
    return jnp.where(rows < n - k, pltpu.roll(x, n - k, axis=0), 0.0)


def _conv_fwd(u, conv_w, L, CC, name):
    T = u.shape[0]
    tc = _tile(CC, (256, 128))
    n = CC // tc

    def body(b_ref, c_ref, x_ref, w_ref, o_ref):
        cx = c_ref[...] * x_ref[...]
        w = w_ref[...]
        conv = w[0:1] * _down(cx, 2) + w[1:2] * _down(cx, 1) + w[2:3] * cx
        o_ref[...] = (b_ref[...] * conv).astype(BF16)

    blk = lambda off: pl.BlockSpec((L, tc), lambda s, j: (s, j + off * n))
    return pl.pallas_call(
        body, name=name, grid=(T // L, n),
        in_specs=[blk(0), blk(1), blk(2), pl.BlockSpec((3, tc), lambda s, j: (0, j))],
        out_specs=pl.BlockSpec((L, tc), lambda s, j: (s, j)),
        out_shape=jax.ShapeDtypeStruct((T, CC), BF16), compiler_params=_params(),
    )(u, u, u, conv_w)


def _conv_bwd(u, dy, conv_w, L, CC, name):
    T = u.shape[0]
    tc = _tile(CC, (256, 128))
    n = CC // tc

    def body(b_ref, c_ref, x_ref, dy_ref, w_ref, db_ref, dc_ref, dx_ref, dw_ref):
        s = pl.program_id(1)
        c_, x_ = c_ref[...], x_ref[...]
        cx = c_ * x_
        w = w_ref[...]
        cx1, cx2 = _down(cx, 1), _down(cx, 2)
        conv = w[0:1] * cx2 + w[1:2] * cx1 + w[2:3] * cx
        dy_ = dy_ref[...]
        db_ref[...] = (dy_ * conv).astype(BF16)
        dconv = dy_ * b_ref[...]
        dcx = w[2:3] * dconv + w[1:2] * _up(dconv, 1) + w[0:1] * _up(dconv, 2)
        dc_ref[...] = (dcx * x_).astype(BF16)
        dx_ref[...] = (dcx * c_).astype(BF16)

        @pl.when(s == 0)
        def _():
            dw_ref[...] = jnp.zeros_like(dw_ref)

        dw_ref[0:1, :] += jnp.sum(dconv * cx2, axis=0, keepdims=True)
        dw_ref[1:2, :] += jnp.sum(dconv * cx1, axis=0, keepdims=True)
        dw_ref[2:3, :] += jnp.sum(dconv * cx, axis=0, keepdims=True)

    blk = lambda off: pl.BlockSpec((L, tc), lambda j, s: (s, j + off * n))
    out = pl.BlockSpec((L, tc), lambda j, s: (s, j))
    act = jax.ShapeDtypeStruct((T, CC), BF16)
    return pl.pallas_call(
        body, name=name, grid=(n, T // L),
        in_specs=[blk(0), blk(1), blk(2), blk(0), pl.BlockSpec((3, tc), lambda j, s: (0, j))],
        out_specs=[out, out, out, pl.BlockSpec((3, tc), lambda j, s: (0, j))],
        out_shape=[act, act, act, jax.ShapeDtypeStruct((3, CC), F32)], compiler_params=_params(),
    )(u, u, u, dy, conv_w)


def _pool_counts(shape, g):
    rows = lax.broadcasted_iota(jnp.int32, shape, 0)
    win = jnp.left_shift(jnp.int32(POOL_WINDOWS[0]), g)
    return jnp.minimum(rows + 1, win).astype(F32)


def _pick_group(g, vals):
    out = vals[-1]
    for i in range(len(vals) - 2, -1, -1):
        out = jnp.where(g == i, vals[i], out)
    return out


def _pool_fwd(u, pool_w, pool_scale, L, CC, PG, name):
    T = u.shape[0]
    G = len(POOL_WINDOWS)
    off = 3 * CC // PG

    def body(z_ref, w_ref, sc_ref, y_ref, d_ref):
        g = pl.program_id(1)
        z = z_ref[...]
        s1 = z + _down(z, 1)
        s2 = s1 + _down(s1, 2)
        s3 = s2 + _down(s2, 4)
        s4 = s3 + _down(s3, 8)
        pooled = _pick_group(g, [s1, s2, s3, s4]) / _pool_counts(z.shape, g)
        d = (pooled - z).astype(BF16)
        d_ref[...] = d
        q = jnp.dot(d, w_ref[...], preferred_element_type=F32)
        y_ref[...] = (q * sc_ref[...]).astype(BF16)

    blk = pl.BlockSpec((L, PG), lambda s, g: (s, g))
    act = jax.ShapeDtypeStruct((T, G * PG), BF16)
    return pl.pallas_call(
        body, name=name, grid=(T // L, G),
        in_specs=[pl.BlockSpec((L, PG), lambda s, g: (s, off + g)), pl.BlockSpec((None, PG, PG), lambda s, g: (g, 0, 0)),
                  pl.BlockSpec((1, PG), lambda s, g: (0, g))],
        out_specs=[blk, blk], out_shape=[act, act], compiler_params=_params(),
    )(u, pool_w, pool_scale)


def _pool_bwd(d, dy, pool_w, pool_scale, L, CC, PG, name):
    T = d.shape[0]
    G = len(POOL_WINDOWS)
    off = CC // PG

    def body(d_ref, dy_ref, w_ref, sc_ref, dz_ref, dw_ref, dsc_ref):
        g = pl.program_id(0)
        s = pl.program_id(1)
        d_ = d_ref[...]
        w = w_ref[...]
        dy_ = dy_ref[...]
        q = jnp.dot(d_, w, preferred_element_type=F32)
        dq = (dy_ * sc_ref[...]).astype(BF16)
        dd = lax.dot_general(dq, w, (((1,), (1,)), ((), ())), preferred_element_type=F32)
        e = dd / _pool_counts(dd.shape, g)
        a1 = e + _up(e, 1)
        a2 = a1 + _up(a1, 2)
        a3 = a2 + _up(a2, 4)
        a4 = a3 + _up(a3, 8)
        dz_ref[...] = (_pick_group(g, [a1, a2, a3, a4]) - dd).astype(BF16)

        @pl.when(s == 0)
        def _():
            dw_ref[...] = jnp.zeros_like(dw_ref)
            dsc_ref[...] = jnp.zeros_like(dsc_ref)

        dw_ref[...] += lax.dot_general(d_, dq, (((0,), (0,)), ((), ())), preferred_element_type=F32)
        dsc_ref[...] += jnp.sum(dy_ * q, axis=0, keepdims=True)

    return pl.pallas_call(
        body, name=name, grid=(G, T // L),
        in_specs=[pl.BlockSpec((L, PG), lambda g, s: (s, g)), pl.BlockSpec((L, PG), lambda g, s: (s, off + g)),
                  pl.BlockSpec((None, PG, PG), lambda g, s: (g, 0, 0)), pl.BlockSpec((1, PG), lambda g, s: (0, g))],
        out_specs=[pl.BlockSpec((L, PG), lambda g, s: (s, g)), pl.BlockSpec((None, PG, PG), lambda g, s: (g, 0, 0)),
                   pl.BlockSpec((1, PG), lambda g, s: (0, g))],
        out_shape=[jax.ShapeDtypeStruct((T, G * PG), BF16), jax.ShapeDtypeStruct((G, PG, PG), F32),
                   jax.ShapeDtypeStruct((1, G * PG), F32)],
        compiler_params=_params(),
    )(d, dy, pool_w, pool_scale)


def _input_grad(dr1, dh_branch, L, name):
    T, D = dr1.shape
    nseq = T // L
    tc = _tile(D, (512, 256, 128))

    def body(a_ref, b_ref, gx_ref, gm_ref):
        dh = ALPHA * a_ref[...] + b_ref[...]
        gm_ref[...] = dh[:N_META]
        gx_ref[...] = dh[N_META:]

    blk = pl.BlockSpec((L, tc), lambda s, j: (s, j))
    return pl.pallas_call(
        body, name=name, grid=(nseq, D // tc), in_specs=[blk, blk],
        out_specs=[pl.BlockSpec((None, L - N_META, tc), lambda s, j: (s, 0, j)),
                   pl.BlockSpec((None, N_META, tc), lambda s, j: (s, 0, j))],
        out_shape=[jax.ShapeDtypeStruct((nseq, L - N_META, D), F32), jax.ShapeDtypeStruct((nseq, N_META, D), F32)],
        compiler_params=_params(),
    )(dr1, dh_branch)


def _chip():
    return 2 * lax.axis_index("x") + lax.axis_index("y")


def _core():
    return lax.axis_index("c")


def _cast_into_full(w, sh, name):
    tr = _rows_tile(sh.srows, sh.scols)
    nb = sh.srows // tr

    def body(w_ref, o_ref):
        o_ref[...] = w_ref[...].astype(BF16)

    if sh.axis == 0:
        out = pl.BlockSpec((tr, sh.scols), lambda i: (_chip() * nb + i, 0))
    else:
        out = pl.BlockSpec((tr, sh.scols), lambda i: (i, _chip()))
    return pl.pallas_call(
        body, name=name, grid=(nb,), in_specs=[pl.BlockSpec((tr, sh.scols), lambda i: (i, 0))], out_specs=out,
        out_shape=jax.ShapeDtypeStruct((sh.rows, sh.cols), BF16), compiler_params=_params(),
    )(w)


def _add_halves(dw, theirs, sh, name):
    R, C = sh.half_all_shape
    tr = _rows_tile(sh.h, C)
    nh, ns = sh.h // tr, sh.srows // tr

    def body(a_ref, b_ref, o_ref):
        o_ref[...] = (a_ref[...].astype(F32) + b_ref[...].astype(F32)).astype(BF16)

    blk = pl.BlockSpec((tr, C), lambda i: (i, 0))
    mine = pl.BlockSpec((tr, C), lambda i: ((i // nh) * ns + _core() * nh + i % nh, 0))
    return pl.pallas_call(body, name=name, grid=(R // tr,), in_specs=[mine, blk], out_specs=blk,
                          out_shape=jax.ShapeDtypeStruct((R, C), BF16), compiler_params=_params())(dw, theirs)


def _sum_parts(part, recv, sh, name):
    h, C = sh.half_shape
    tr = _rows_tile(h, C)
    nh = h // tr

    def body(o_ref, r_ref, out_ref):
        acc = o_ref[...].astype(F32)
        for k in range(3):
            acc = acc + r_ref[k].astype(F32)
        out_ref[...] = acc

    if sh.axis == 0:
        own = pl.BlockSpec((tr, C), lambda i: (_chip() * nh + i, 0))
    else:
        own = pl.BlockSpec((tr, C), lambda i: (i, _chip()))
    return pl.pallas_call(
        body, name=name, grid=(nh,), in_specs=[own, pl.BlockSpec((3, tr, C), lambda i: (0, i, 0))],
        out_specs=pl.BlockSpec((tr, C), lambda i: (_core() * nh + i, 0)),
        out_shape=jax.ShapeDtypeStruct((sh.srows, sh.scols), F32), compiler_params=_params(),
    )(part, recv)


def _adamw(w, g, m, v, name):
    R, C = w.shape
    tr = _rows_tile(R, C, mult=8)

    def body(w_ref, g_ref, m_ref, v_ref, d_ref, nm_ref, nv_ref):
        g_ = g_ref[...]
        m_ = ADAM_B1 * m_ref[...] + (1.0 - ADAM_B1) * g_
        v_ = ADAM_B2 * v_ref[...] + (1.0 - ADAM_B2) * (g_ * g_)
        m_hat = m_ / (1.0 - ADAM_B1 ** ADAM_STEP)
        v_hat = v_ / (1.0 - ADAM_B2 ** ADAM_STEP)
        d_ref[...] = -ADAM_LR * (m_hat / (jnp.sqrt(v_hat) + ADAM_EPS) + ADAM_WD * w_ref[...])
        nm_ref[...] = m_
        nv_ref[...] = v_

    blk = pl.BlockSpec((tr, C), lambda i: (i, 0))
    shp = jax.ShapeDtypeStruct((R, C), F32)
    return pl.pallas_call(body, name=name, grid=(R // tr,), in_specs=[blk] * 4, out_specs=[blk] * 3,
                          out_shape=[shp] * 3, compiler_params=_params())(w, g, m, v)


def _pos():
    return lax.axis_index("x"), lax.axis_index("y"), lax.axis_index("c")


def _other_chips(x, y):
    return [(1 - x, y), (x, 1 - y), (1 - x, 1 - y)]


class _Sharded:
    def __init__(self, full_shape, axis):
        self.rows, self.cols = full_shape
        self.axis = axis
        self.srows = self.rows // 4 if axis == 0 else self.rows
        self.scols = self.cols if axis == 0 else self.cols // 4
        assert self.srows % 2 == 0
        self.h = self.srows // 2
        self.half_all_shape = (4 * self.h, self.cols) if axis == 0 else (self.h, self.cols)
        self.half_shape = (self.h, self.scols)

    def shard(self, ref, q):
        if self.axis == 0:
            return ref.at[pl.ds(q * self.srows, self.srows), :]
        return ref.at[:, pl.ds(pl.multiple_of(q * self.scols, LANES), self.scols)]

    def half(self, ref, q, c):
        if self.axis == 0:
            return ref.at[pl.ds(q * self.srows + c * self.h, self.h), :]
        return ref.at[pl.ds(c * self.h, self.h), pl.ds(pl.multiple_of(q * self.scols, LANES), self.scols)]

    def half_in_stack(self, ref, q):
        if self.axis == 0:
            return ref.at[pl.ds(q * self.h, self.h), :]
        return ref.at[:, pl.ds(pl.multiple_of(q * self.scols, LANES), self.scols)]


ANY = pl.BlockSpec(memory_space=pl.ANY)
DMA_ROW_ALIGN = 16
CHUNK_COUNTS = (16, 8, 43, 4, 2)


def _row_chunks(rows):
    n = 1
    if rows % DMA_ROW_ALIGN == 0:
        n = _tile(rows // DMA_ROW_ALIGN, CHUNK_COUNTS)
        n = n if n in CHUNK_COUNTS else 1
    size = rows // n
    return [(k * size, size) for k in range(n)]


class _Pieces:
    def __init__(self, src, dst, send_sem, recv_sem, to):
        self.args = (src, dst, send_sem, recv_sem, to)

    def _copy(self, rows=None):
        src, dst, send_sem, recv_sem, to = self.args
        if rows is not None:
            src, dst = src.at[pl.ds(*rows), :], dst.at[pl.ds(*rows), :]
        return pltpu.make_async_remote_copy(src_ref=src, dst_ref=dst, send_sem=send_sem, recv_sem=recv_sem,
                                            device_id=to, device_id_type=MESH)

    def start(self):
        for rows in _row_chunks(self.args[0].shape[0]):
            self._copy(rows).start()

    def wait_send(self):
        self._copy().wait_send()

    def wait_recv(self):
        self._copy().wait_recv()

    def wait(self):
        self._copy().wait()


HBM = pl.BlockSpec(memory_space=pltpu.HBM)
SEM = pl.BlockSpec(memory_space=pltpu.SEMAPHORE)
SPLIT_COPY = pltpu.CompilerParams(has_side_effects=pltpu.SideEffectType.DATAFLOW_SIDE_EFFECTING)
TOKEN = jax.ShapeDtypeStruct((8, LANES), F32)


def _in_hbm(a):
    return pltpu.with_memory_space_constraint(a, pltpu.HBM)


def _gather_start(full, sh, name):
    def body(full_ref, send_sems, recv_sems, thru_ref, token):
        x, y, c = _pos()
        mine = sh.half(full_ref, 2 * x + y, c)
        for j, (cx, cy) in enumerate(_other_chips(x, y)):
            _Pieces(mine, mine, send_sems.at[j], recv_sems.at[j], (cx, cy, c)).start()
        token[...] = jnp.zeros_like(token)

    return pl.pallas_call(
        body, name=name, in_specs=(HBM,),
        out_shape=(pltpu.SemaphoreType.DMA((3,)), pltpu.SemaphoreType.DMA((3,)), pltpu.HBM(full.shape, full.dtype), TOKEN),
        out_specs=(SEM, SEM, HBM, pl.BlockSpec(memory_space=pltpu.VMEM)), input_output_aliases={0: 2},
        compiler_params=SPLIT_COPY,
    )(_in_hbm(full))


def _gather_wait(send_sems, recv_sems, full, after, sh, name):
    def body(full_ref, send_sems, recv_sems, after_ref, out_ref):
        x, y, c = _pos()
        mine = sh.half(full_ref, 2 * x + y, c)
        for j, (cx, cy) in enumerate(_other_chips(x, y)):
            cp = pltpu.make_async_remote_copy(src_ref=mine, dst_ref=sh.half(full_ref, 2 * cx + cy, c),
                                              send_sem=send_sems.at[j], recv_sem=recv_sems.at[j],
                                              device_id=(cx, cy, c), device_id_type=MESH)
            cp.wait_send()
            cp.wait_recv()

    return pl.pallas_call(
        body, name=name, in_specs=(HBM, SEM, SEM, ANY), out_shape=pltpu.HBM(full.shape, full.dtype), out_specs=HBM,
        input_output_aliases={0: 0}, compiler_params=SPLIT_COPY,
    )(full, send_sems, recv_sems, after)


def _gather_forward(full, sh, name):
    def body(full_ref, out_ref, send_sems, recv_sems):
        x, y, c = _pos()
        sib = (x, y, 1 - c)
        chips = _other_chips(x, y)
        passed = []
        for j, (cx, cy) in enumerate(chips):
            landed_src = sh.half(full_ref, 2 * cx + cy, c)
            landed_dst = sh.half(out_ref, 2 * cx + cy, c)
            cp = _Pieces(landed_src, landed_dst, send_sems.at[j], recv_sems.at[j], sib)
            cp.start()
            passed.append(cp)
        for j, (cx, cy) in enumerate(chips):
            theirs = sh.half(out_ref, 2 * cx + cy, 1 - c)
            _Pieces(theirs, theirs, send_sems.at[j], recv_sems.at[j], sib).wait_recv()
        for cp in passed:
            cp.wait_send()

    return pl.pallas_call(
        body, name=name, in_specs=[ANY], out_specs=ANY, out_shape=jax.ShapeDtypeStruct(full.shape, full.dtype),
        input_output_aliases={0: 0},
        scratch_shapes=[pltpu.SemaphoreType.DMA((3,)), pltpu.SemaphoreType.DMA((3,))],
    )(full)


def _rs_sibling(dw, sh, name):
    n = 4 if sh.axis == 0 else 1

    def body(dw_ref, theirs_ref, send_sems, recv_sems):
        x, y, c = _pos()
        sib = (x, y, 1 - c)
        cps = []
        for k in range(n):
            if sh.axis == 0:
                give = dw_ref.at[pl.ds(k * sh.srows + (1 - c) * sh.h, sh.h), :]
                theirs_k = theirs_ref.at[pl.ds(k * sh.h, sh.h), :]
            else:
                give = dw_ref.at[pl.ds((1 - c) * sh.h, sh.h), :]
                theirs_k = theirs_ref
            cps.append(_Pieces(give, theirs_k, send_sems.at[k], recv_sems.at[k], sib))
        for cp in cps:
            cp.start()
        for cp in cps:
            cp.wait()

    return pl.pallas_call(
        body, name=name, in_specs=[ANY], out_specs=ANY, out_shape=jax.ShapeDtypeStruct(sh.half_all_shape, dw.dtype),
        scratch_shapes=[pltpu.SemaphoreType.DMA((n,)), pltpu.SemaphoreType.DMA((n,))],
    )(dw)


def _rs_start(part, sh, name):
    land_shape = (3,) + sh.half_shape

    def body(p_ref, land_ref, send_sems, recv_sems, p_thru, land_thru, token):
        x, y, c = _pos()
        for j, (cx, cy) in enumerate(_other_chips(x, y)):
            _Pieces(sh.half_in_stack(p_ref, 2 * cx + cy), land_ref.at[j], send_sems.at[j], recv_sems.at[j],
                    (cx, cy, c)).start()
        token[...] = jnp.zeros_like(token)

    return pl.pallas_call(
        body, name=name, in_specs=(HBM, HBM),
        out_shape=(pltpu.SemaphoreType.DMA((3,)), pltpu.SemaphoreType.DMA((3,)), pltpu.HBM(part.shape, part.dtype),
                   pltpu.HBM(land_shape, part.dtype), TOKEN),
        out_specs=(SEM, SEM, HBM, HBM, pl.BlockSpec(memory_space=pltpu.VMEM)), input_output_aliases={0: 2, 1: 3},
        compiler_params=SPLIT_COPY,
    )(_in_hbm(part), _in_hbm(lax.empty(land_shape, part.dtype)))


def _rs_wait(send_sems, recv_sems, part, land, after, sh, name):
    def body(p_ref, land_ref, send_sems, recv_sems, after_ref, p_out, land_out):
        x, y, c = _pos()
        for j, (cx, cy) in enumerate(_other_chips(x, y)):
            cp = pltpu.make_async_remote_copy(src_ref=sh.half_in_stack(p_ref, 2 * cx + cy), dst_ref=land_ref.at[j],
                                              send_sem=send_sems.at[j], recv_sem=recv_sems.at[j],
                                              device_id=(cx, cy, c), device_id_type=MESH)
            cp.wait_send()
            cp.wait_recv()

    return pl.pallas_call(
        body, name=name, in_specs=(HBM, HBM, SEM, SEM, ANY),
        out_shape=(pltpu.HBM(part.shape, part.dtype), pltpu.HBM(land.shape, land.dtype)), out_specs=(HBM, HBM),
        input_output_aliases={0: 0, 1: 1}, compiler_params=SPLIT_COPY,
    )(part, land, send_sems, recv_sems, after)


def _share_halves(g, sh, name):
    def body(g_ref, out_ref, send_sem, recv_sem):
        x, y, c = _pos()
        cp = _Pieces(g_ref.at[pl.ds(c * sh.h, sh.h), :], out_ref.at[pl.ds(c * sh.h, sh.h), :], send_sem, recv_sem,
                     (x, y, 1 - c))
        cp.start()
        theirs = out_ref.at[pl.ds((1 - c) * sh.h, sh.h), :]
        _Pieces(theirs, theirs, send_sem, recv_sem, (x, y, 1 - c)).wait_recv()
        cp.wait_send()

    return pl.pallas_call(
        body, name=name, in_specs=[ANY], out_specs=ANY, out_shape=jax.ShapeDtypeStruct(g.shape, g.dtype),
        input_output_aliases={0: 0}, scratch_shapes=[pltpu.SemaphoreType.DMA, pltpu.SemaphoreType.DMA],
    )(g)


def _exchange_small(v, reduce, name):
    R, C = v.shape

    def body(v_ref, o_ref, buf, send_sems, recv_sems):
        x, y, c = _pos()
        me = 4 * x + 2 * y + c
        slots = buf if reduce else o_ref
        slots[me] = v_ref[...]
        cps = []
        for k in range(1, 8):
            px = 1 - x if k & 4 else x
            py = 1 - y if k & 2 else y
            pc = 1 - c if k & 1 else c
            cps.append((_Pieces(v_ref, slots.at[me], send_sems.at[k - 1], recv_sems.at[k - 1], (px, py, pc)),
                        4 * px + 2 * py + pc))
        for cp, _ in cps:
            cp.start()
        for k, (cp, peer) in enumerate(cps):
            pltpu.make_async_remote_copy(
                src_ref=v_ref, dst_ref=slots.at[peer], send_sem=send_sems.at[k], recv_sem=recv_sems.at[k],
                device_id=(x, y, c), device_id_type=MESH).wait_recv()
        if reduce:
            acc = buf[0]
            for d in range(1, 8):
                acc = acc + buf[d]
            o_ref[...] = acc
        for cp, _ in cps:
            cp.wait_send()

    vm = pl.BlockSpec(memory_space=pltpu.VMEM)
    out_shape = jax.ShapeDtypeStruct((R, C) if reduce else (8, R, C), v.dtype)
    scratch = [pltpu.VMEM((8, R, C) if reduce else (8, LANES), v.dtype),
               pltpu.SemaphoreType.DMA((7,)), pltpu.SemaphoreType.DMA((7,))]
    return pl.pallas_call(body, name=name, in_specs=[vm], out_specs=vm, out_shape=out_shape, scratch_shapes=scratch,
                          compiler_params=_params())(v)


def _pack_rows(parts, width):
    rows, offs, at = [], [], 0
    for p in parts:
        flat = p.reshape(-1)
        n = 8 * -(-flat.shape[0] // (8 * width))
        flat = jnp.pad(flat, (0, n * width - flat.shape[0]))
        rows.append(flat.reshape(n, width))
        offs.append(at)
        at += n
    return jnp.concatenate(rows, axis=0), offs


def _unpack_rows(packed, offs, shapes):
    out = []
    for off, shp in zip(offs, shapes):
        size = 1
        for s in shp:
            size *= s
        n = -(-size // packed.shape[1])
        out.append(packed[off:off + n].reshape(-1)[:size].reshape(shp))
    return out


def kernel(x, meta_tokens, ffn1_w_gu, ffn1_w_down, ln1_g, ln1_b, w_in, conv_w, pool_w, pool_scale, w_out, ln2_g, ln2_b, ffn2_w_gu, ffn2_w_down, ln3_g, ln3_b, loss_target, m_meta_tokens, m_ffn1_w_gu, m_ffn1_w_down, m_ln1_g, m_ln1_b, m_w_in, m_conv_w, m_pool_w, m_pool_scale, m_w_out, m_ln2_g, m_ln2_b, m_ffn2_w_gu, m_ffn2_w_down, m_ln3_g, m_ln3_b, v_meta_tokens, v_ffn1_w_gu, v_ffn1_w_down, v_ln1_g, v_ln1_b, v_w_in, v_conv_w, v_pool_w, v_pool_scale, v_w_out, v_ln2_g, v_ln2_b, v_ffn2_w_gu, v_ffn2_w_down, v_ln3_g, v_ln3_b):
    nseq, S, D = x.shape
    L = S + N_META
    T = nseq * L
    F = ffn1_w_down.shape[1] * 4
    CC = conv_w.shape[2] * 4
    G, PGs, PG = pool_w.shape[1:]
    PC = G * PG
    IN = w_in.shape[2] * 4
    qx, qy, qc = _pos()
    q = 2 * qx + qy

    sh_gu = _Sharded((D, 2 * F), 1)
    sh_down = _Sharded((F, D), 0)
    sh_in = _Sharded((D, IN), 1)
    sh_out = _Sharded((CC + PC, D), 0)

    gathers = {}
    tok = jnp.zeros((), F32)
    for tag, w, sh in (("gu1", ffn1_w_gu, sh_gu), ("d1", ffn1_w_down, sh_down), ("in", w_in, sh_in),
                       ("out", w_out, sh_out), ("gu2", ffn2_w_gu, sh_gu), ("d2", ffn2_w_down, sh_down)):
        full = _cast_into_full(w[0], sh, "cast_" + tag)
        s_, r_, thru, t_ = _gather_start(full, sh, "ag_start_" + tag)
        gathers[tag] = (s_, r_, thru, sh)
        tok = tok + t_[0, 0]

    def arrive(tag, after):
        s_, r_, thru, sh = gathers.pop(tag)
        full = _gather_wait(s_, r_, thru, after, sh, "ag_wait_" + tag)
        return _gather_forward(full, sh, "ag_fwd_" + tag)

    small_w = [meta_tokens, conv_w[0], pool_w[0]]
    packed, offs = _pack_rows(small_w, 4 * LANES)
    slots = _exchange_small(packed, False, "ag_small")
    per_chip = [_unpack_rows(slots[2 * k], offs, [p.shape for p in small_w]) for k in range(4)]
    meta_full = jnp.concatenate([p[0] for p in per_chip], axis=1)
    conv_full = jnp.concatenate([p[1] for p in per_chip], axis=1)
    poolw_full = jnp.concatenate([p[2] for p in per_chip], axis=1).astype(BF16)
    pscale = pool_scale

    h0 = jnp.concatenate([jnp.broadcast_to(meta_full[None], (nseq, N_META, D)), x], axis=1).reshape(T, D)
    h0b = (h0 + tok).astype(BF16)
    tgt = jnp.pad(loss_target, ((0, 0), (N_META, 0), (0, 0))).reshape(T, D)
    mask = (lax.broadcasted_iota(jnp.int32, (nseq, L, 1), 1) >= N_META).astype(F32).reshape(T, 1)

    wgu1 = arrive("gu1", h0b)
    gu1 = _mm_nn(h0b, wgu1, BF16, "ffn_gu")
    wd1 = arrive("d1", gu1)
    a1 = _silu_mul(gu1, "silu_mul")
    f1 = _mm_nn(a1, wd1, F32, "ffn_down")
    win = arrive("in", f1)
    h1, h1b, xh1, rs1 = _ln_fwd(h0, f1, ln1_g, ln1_b, 0.5, "ln_fwd")

    u = _mm_nn(h1b, win, F32, "mix_in")
    wout = arrive("out", u)
    yc = _conv_fwd(u, conv_full, L, CC, "conv_fwd")
    yp, dpool = _pool_fwd(u, poolw_full, pscale, L, CC, PG, "pool_fwd")
    ymix = jnp.concatenate([yc, yp], axis=1)
    o = _mm_nn(ymix, wout, F32, "mix_out")
    wgu2 = arrive("gu2", o)
    h2, h2b, xh2, rs2 = _ln_fwd(h1, o, ln2_g, ln2_b, 1.0, "ln_fwd")

    gu2 = _mm_nn(h2b, wgu2, BF16, "ffn_gu")
    wd2 = arrive("d2", gu2)
    a2 = _silu_mul(gu2, "silu_mul")
    f2 = _mm_nn(a2, wd2, F32, "ffn_down")
    dr3, dr3b, gs3, bs3, lsum = _ln3_loss(h2, f2, tgt, mask, ln3_g, ln3_b, "ln3_loss")
    loss = lax.psum(0.5 * jnp.sum(lsum) / D, ALL_AXES)

    pending = {}

    def rs_begin(dw, sh, tag):
        theirs = _rs_sibling(dw, sh, "rs_sib_" + tag)
        part = _add_halves(dw, theirs, sh, "rs_add_" + tag)
        s_, r_, p_thru, land, t_ = _rs_start(part, sh, "rs_start_" + tag)
        pending[tag] = (s_, r_, p_thru, land, sh)
        return t_

    def rs_end(tag, after):
        s_, r_, p_thru, land, sh = pending.pop(tag)
        part, recv = _rs_wait(s_, r_, p_thru, land, after, sh, "rs_wait_" + tag)
        g = _sum_parts(part, recv, sh, "rs_sum_" + tag)
        return _share_halves(g, sh, "rs_share_" + tag)

    def ffn_bwd(drb, a, gu, hb, wgu, wd, sfx, after):
        da = _mm_nt(drb, wd, BF16, "ffn_da", after=after)
        dwd = _mm_tn(a, drb, "ffn_dwd")
        t_d = rs_begin(dwd, sh_down, "d" + sfx)
        dgu = _swiglu_bwd(da, gu, "swiglu_bwd")
        dwgu = _mm_tn(hb, dgu, "ffn_dwgu", after=t_d)
        t_gu = rs_begin(dwgu, sh_gu, "gu" + sfx)
        return _mm_nt(dgu, wgu, F32, "ffn_dh", after=t_gu)

    dh2 = ffn_bwd(dr3b, a2, gu2, h2b, wgu2, wd2, "2", None)

    dr2, dr2b, gs2, bs2 = _ln_bwd(dr3, dh2, xh2, rs2, ln2_g, 1.0, "ln_bwd")
    dymix = _mm_nt(dr2b, wout, F32, "mix_dy")
    dwout = _mm_tn(ymix, dr2b, "mix_dwout")
    t_out = rs_begin(dwout, sh_out, "out")
    db_, dc_, dx_, dconvw = _conv_bwd(u, dymix, conv_full, L, CC, "conv_bwd")
    dz_, dpoolw, dpscale = _pool_bwd(dpool, dymix, poolw_full, pscale, L, CC, PG, "pool_bwd")
    du = jnp.concatenate([db_, dc_, dx_, dz_], axis=1)
    dwin = _mm_tn(h1b, du, "mix_dwin", after=t_out)
    t_in = rs_begin(dwin, sh_in, "in")
    dh1 = _mm_nt(du, win, F32, "mix_dh", after=t_in)

    dr1, dr1b, gs1, bs1 = _ln_bwd(dr2, dh1, xh1, rs1, ln1_g, 0.5, "ln_bwd")
    dh0f = ffn_bwd(dr1b, a1, gu1, h0b, wgu1, wd1, "1", None)
    grad_x, gmeta = _input_grad(dr1, dh0f, L, "input_grad")

    g_d2 = rs_end("d2", grad_x)
    g_gu2 = rs_end("gu2", g_d2)
    g_out = rs_end("out", g_gu2)
    g_in = rs_end("in", g_out)
    g_d1 = rs_end("d1", g_in)
    g_gu1 = rs_end("gu1", g_d1)

    small_g = [gs1, bs1, gs2, bs2, gs3, bs3, dpscale, dconvw, jnp.sum(gmeta, axis=0)]
    gpacked, goffs = _pack_rows(small_g, D)
    gsum = _exchange_small(gpacked, True, "ar_small")
    (g_ln1g, g_ln1b, g_ln2g, g_ln2b, g_ln3g, g_ln3b, g_pscale, g_conv_all, g_meta_all) = _unpack_rows(
        gsum, goffs, [p.shape for p in small_g])
    g_pool_all = _exchange_small(dpoolw.reshape(-1, PG), True, "ar_pool").reshape(G, PG, PG)
    g_meta = lax.dynamic_slice_in_dim(g_meta_all, q * (D // 4), D // 4, axis=1)
    g_conv = lax.dynamic_slice_in_dim(g_conv_all, q * (CC // 4), CC // 4, axis=1)
    g_pool = lax.dynamic_slice_in_dim(g_pool_all, q * PGs, PGs, axis=1)

    names = ["meta_tokens", "ffn1_w_gu", "ffn1_w_down", "ln1_g", "ln1_b", "w_in", "conv_w", "pool_w", "pool_scale",
             "w_out", "ln2_g", "ln2_b", "ffn2_w_gu", "ffn2_w_down", "ln3_g", "ln3_b"]
    ws = dict(zip(names, [meta_tokens, ffn1_w_gu, ffn1_w_down, ln1_g, ln1_b, w_in, conv_w, pool_w, pool_scale, w_out,
                          ln2_g, ln2_b, ffn2_w_gu, ffn2_w_down, ln3_g, ln3_b]))
    ms = dict(zip(names, [m_meta_tokens, m_ffn1_w_gu, m_ffn1_w_down, m_ln1_g, m_ln1_b, m_w_in, m_conv_w, m_pool_w,
                          m_pool_scale, m_w_out, m_ln2_g, m_ln2_b, m_ffn2_w_gu, m_ffn2_w_down, m_ln3_g, m_ln3_b]))
    vs = dict(zip(names, [v_meta_tokens, v_ffn1_w_gu, v_ffn1_w_down, v_ln1_g, v_ln1_b, v_w_in, v_conv_w, v_pool_w,
                          v_pool_scale, v_w_out, v_ln2_g, v_ln2_b, v_ffn2_w_gu, v_ffn2_w_down, v_ln3_g, v_ln3_b]))
    grads = {
        "meta_tokens": g_meta, "ffn1_w_gu": g_gu1[None], "ffn1_w_down": g_d1[None], "ln1_g": g_ln1g, "ln1_b": g_ln1b,
        "w_in": g_in[None], "conv_w": g_conv[None], "pool_w": g_pool[None], "pool_scale": g_pscale,
        "w_out": g_out[None], "ln2_g": g_ln2g, "ln2_b": g_ln2b, "ffn2_w_gu": g_gu2[None], "ffn2_w_down": g_d2[None],
        "ln3_g": g_ln3g, "ln3_b": g_ln3b,
    }
    big = ("ffn1_w_gu", "ffn1_w_down", "w_in", "w_out", "ffn2_w_gu", "ffn2_w_down")
    delta, new_m, new_v = {}, {}, {}
    for n in big:
        d_, m_, v_ = _adamw(ws[n][0], grads[n][0], ms[n][0], vs[n][0], "adamw_" + n.split("_", 1)[1])
        delta[n], new_m[n], new_v[n] = d_[None], m_[None], v_[None]
    small = [n for n in names if n not in big]
    width = 4 * LANES
    pw, poffs = _pack_rows([ws[n] for n in small], width)
    pg, _ = _pack_rows([grads[n] for n in small], width)
    pm, _ = _pack_rows([ms[n] for n in small], width)
    pv, _ = _pack_rows([vs[n] for n in small], width)
    sd, sm, sv = _adamw(pw, pg, pm, pv, "adamw_small")
    shapes = [ws[n].shape for n in small]
    for n, d_, m_, v_ in zip(small, _unpack_rows(sd, poffs, shapes), _unpack_rows(sm, poffs, shapes),
                             _unpack_rows(sv, poffs, shapes)):
        delta[n], new_m[n], new_v[n] = d_, m_, v_

    return (loss, grad_x, *[grads[n] for n in names], *[delta[n] for n in names], *[new_m[n] for n in names],
            *[new_v[n] for n in names])
```

```python
import jax
import jax.numpy as jnp
from jax import lax
from jax.experimental import pallas as pl
from jax.experimental.pallas import tpu as pltpu

F32 = jnp.float32
BF16 = jnp.bfloat16
MESH = pl.DeviceIdType.MESH
ALL_AXES = ("x", "y", "c")

N_META = 16
POOL_WINDOWS = (2, 4, 8, 16)
LN_EPS = 1e-5
ALPHA = 2.0 ** 0.25
ADAM_LR, ADAM_B1, ADAM_B2, ADAM_EPS, ADAM_WD, ADAM_STEP = 0.001, 0.9, 0.999, 1e-08, 0.01, 10

VMEM_LIMIT_V7X = 60 * 1024 * 1024
LANES = 128
ELEMENTWISE_BLOCK_BYTES = 2 * 1024 * 1024


def _params(**kw):
    return pltpu.CompilerParams(vmem_limit_bytes=VMEM_LIMIT_V7X, **kw)


def _tile(n, prefs):
    for t in prefs:
        if t <= n and n % t == 0:
            return t
    return n


def _rows_tile(rows, cols, mult=16):
    cap = max(mult, ELEMENTWISE_BLOCK_BYTES // (4 * cols))
    best = None
    for t in range(mult, min(rows, cap) + 1, mult):
        if rows % t == 0:
            best = t
    return best if best is not None else rows


def _matmul(a, b, *, mode, tm, tn, tk, out_dtype, name, a_outer=True, scale=1.0, after=None, k_off=0, nk=None,
            addend=None):
    if mode == "tn":
        K, M = a.shape
    else:
        M, K = a.shape
    N = b.shape[0] if mode == "nt" else b.shape[1]
    assert M % tm == 0 and N % tn == 0 and K % tk == 0, (name, M, N, K, tm, tn, tk)
    nk = K // tk if nk is None else nk
    if a_outer:
        grid = (M // tm, N // tn, nk)
        ij = lambda p, q: (p, q)
    else:
        grid = (N // tn, M // tm, nk)
        ij = lambda p, q: (q, p)

    if mode == "tn":
        a_spec = pl.BlockSpec((tk, tm), lambda p, q, k: (k + k_off, ij(p, q)[0]))
        dims = (((0,), (0,)), ((), ()))
    else:
        a_spec = pl.BlockSpec((tm, tk), lambda p, q, k: (ij(p, q)[0], k + k_off))
        dims = (((1,), (1,)), ((), ())) if mode == "nt" else (((1,), (0,)), ((), ()))
    if mode == "nt":
        b_spec = pl.BlockSpec((tn, tk), lambda p, q, k: (ij(p, q)[1], k + k_off))
    else:
        b_spec = pl.BlockSpec((tk, tn), lambda p, q, k: (k + k_off, ij(p, q)[1]))
    o_spec = pl.BlockSpec((tm, tn), lambda p, q, k: ij(p, q))

    extra, extra_specs = [], []
    if addend is not None:
        extra.append(addend)
        extra_specs.append(o_spec)
    if after is not None:
        extra.append(after)
        extra_specs.append(pl.BlockSpec(memory_space=pl.ANY))
    n_in = 2 + len(extra)

    def fin(acc, refs):
        if scale != 1.0:
            acc = acc * scale
        if addend is not None:
            acc = acc + refs[2][...]
        return acc.astype(out_dtype)

    if nk == 1:
        def body(*refs):
            a_ref, b_ref, o_ref = refs[0], refs[1], refs[n_in]
            o_ref[...] = fin(lax.dot_general(a_ref[...], b_ref[...], dims, preferred_element_type=F32), refs)
        scratch = []
    else:
        def body(*refs):
            a_ref, b_ref, o_ref, acc_ref = refs[0], refs[1], refs[n_in], refs[n_in + 1]
            k = pl.program_id(2)
            prod = lax.dot_general(a_ref[...], b_ref[...], dims, preferred_element_type=F32)

            @pl.when(k == 0)
            def _():
                acc_ref[...] = prod

            @pl.when(k > 0)
            def _():
                acc_ref[...] += prod

            @pl.when(k == nk - 1)
            def _():
                o_ref[...] = fin(acc_ref[...], refs)
        scratch = [pltpu.VMEM((tm, tn), F32)]

    return pl.pallas_call(
        body, name=name, grid=grid, in_specs=[a_spec, b_spec] + extra_specs, out_specs=o_spec,
        out_shape=jax.ShapeDtypeStruct((M, N), out_dtype), scratch_shapes=scratch,
        compiler_params=_params(),
    )(a, b, *extra)


TM_PREFS = (1376, 688, 512, 256, 128, 64)


def _mm_nn(a, b, out_dtype, name, after=None):
    M, K = a.shape
    N = b.shape[1]
    big_k = K * 2 * 1376 > 12 * 1024 * 1024
    tm = _tile(M, (688,) + TM_PREFS[2:]) if big_k else _tile(M, TM_PREFS)
    tn = _tile(N, (256, 128)) if big_k else _tile(N, (512, 256, 128))
    return _matmul(a, b, mode="nn", tm=tm, tn=tn, tk=K, out_dtype=out_dtype, name=name, after=after)


def _mm_nt(a, b, out_dtype, name, scale=1.0, after=None):
    M, K = a.shape
    N = b.shape[0]
    mib = 1024 * 1024
    parts = 1
    while K // parts * 2 * 688 > 16 * mib and K % (2 * parts * LANES) == 0:
        parts *= 2
    kp = K // parts
    tm = _tile(M, TM_PREFS) if kp * 2 * 1376 <= 12 * mib else _tile(M, TM_PREFS[1:])
    tn = _tile(N, (512, 256, 128))
    if 4 * kp * (tm + tn) > 44 * mib:
        tn = _tile(N, (256, 128))
    acc = None
    for p in range(parts):
        last = p == parts - 1
        acc = _matmul(a, b, mode="nt", tm=tm, tn=tn, tk=kp, k_off=p, nk=1, out_dtype=out_dtype if last else F32,
                      name=name, scale=scale, after=after if p == 0 else None, addend=acc)
    return acc


def _mm_tn(a, b, name, after=None):
    T, M = a.shape
    N = b.shape[1]
    if M % 2048 == 0:
        tm, tn, a_outer = 2048, _tile(N, (512, 256, 128)), True
    else:
        tm, tn, a_outer = _tile(M, (256, 128)), _tile(N, (2048, 1024, 512, 256, 128)), False
    return _matmul(a, b, mode="tn", tm=tm, tn=tn, tk=T, out_dtype=BF16, name=name, a_outer=a_outer, after=after)


def _silu_mul(gu, name):
    T, F2 = gu.shape
    F = F2 // 2
    tm = _rows_tile(T, F)

    def body(g_ref, u_ref, o_ref):
        g = g_ref[...].astype(F32)
        o_ref[...] = (g * jax.nn.sigmoid(g) * u_ref[...].astype(F32)).astype(BF16)

    return pl.pallas_call(
        body, name=name, grid=(T // tm,),
        in_specs=[pl.BlockSpec((tm, F), lambda i: (i, 0)), pl.BlockSpec((tm, F), lambda i: (i, 1))],
        out_specs=pl.BlockSpec((tm, F), lambda i: (i, 0)),
        out_shape=jax.ShapeDtypeStruct((T, F), BF16), compiler_params=_params(),
    )(gu, gu)


def _swiglu_bwd(da, gu, name):
    T, F2 = gu.shape
    F = F2 // 2
    tm = _rows_tile(T, F)

    def body(da_ref, g_ref, u_ref, o_ref):
        g = g_ref[...].astype(F32)
        da_ = da_ref[...].astype(F32)
        s = jax.nn.sigmoid(g)
        o_ref[:, :F] = (da_ * u_ref[...].astype(F32) * (s * (1.0 + g * (1.0 - s)))).astype(BF16)
        o_ref[:, F:] = (da_ * (g * s)).astype(BF16)

    return pl.pallas_call(
        body, name=name, grid=(T // tm,),
        in_specs=[pl.BlockSpec((tm, F), lambda i: (i, 0)), pl.BlockSpec((tm, F), lambda i: (i, 0)),
                  pl.BlockSpec((tm, F), lambda i: (i, 1))],
        out_specs=pl.BlockSpec((tm, F2), lambda i: (i, 0)),
        out_shape=jax.ShapeDtypeStruct((T, F2), BF16), compiler_params=_params(),
    )(da, gu, gu)


def _ln_stats(r):
    mu = jnp.mean(r, axis=-1, keepdims=True)
    xc = r - mu
    var = jnp.mean(xc * xc, axis=-1, keepdims=True)
    rstd = lax.rsqrt(var + LN_EPS)
    return xc * rstd, rstd


def _ln_bwd_math(dh, xhat, rstd, g):
    dxh = dh * g
    m1 = jnp.mean(dxh, axis=-1, keepdims=True)
    m2 = jnp.mean(dxh * xhat, axis=-1, keepdims=True)
    return rstd * (dxh - m1 - xhat * m2)


def _ln_fwd(hprev, f, g, b, fscale, name):
    T, D = hprev.shape
    tm = _rows_tile(T, D)

    def body(hp_ref, f_ref, g_ref, b_ref, h_ref, hb_ref, xh_ref, rs_ref):
        r = ALPHA * hp_ref[...] + fscale * f_ref[...]
        xhat, rstd = _ln_stats(r)
        h = xhat * g_ref[...] + b_ref[...]
        h_ref[...] = h
        hb_ref[...] = h.astype(BF16)
        xh_ref[...] = xhat
        rs_ref[...] = rstd

    row = pl.BlockSpec((tm, D), lambda i: (i, 0))
    vec = pl.BlockSpec((1, D), lambda i: (0, 0))
    col = pl.BlockSpec((tm, 1), lambda i: (i, 0))
    return pl.pallas_call(
        body, name=name, grid=(T // tm,), in_specs=[row, row, vec, vec], out_specs=[row, row, row, col],
        out_shape=[jax.ShapeDtypeStruct((T, D), F32), jax.ShapeDtypeStruct((T, D), BF16),
                   jax.ShapeDtypeStruct((T, D), F32), jax.ShapeDtypeStruct((T, 1), F32)],
        compiler_params=_params(),
    )(hprev, f, g, b)


def _ln_bwd(dr_next, dh_branch, xhat, rstd, g, fscale, name):
    T, D = xhat.shape
    tm = _rows_tile(T, D)

    def body(dn_ref, db_ref, xh_ref, rs_ref, g_ref, dr_ref, drb_ref, gs_ref, bs_ref):
        i = pl.program_id(0)
        dh = ALPHA * dn_ref[...] + db_ref[...]
        xhat_ = xh_ref[...]
        dr = _ln_bwd_math(dh, xhat_, rs_ref[...], g_ref[...])
        dr_ref[...] = dr
        drb_ref[...] = (fscale * dr).astype(BF16)

        @pl.when(i == 0)
        def _():
            gs_ref[...] = jnp.zeros_like(gs_ref)
            bs_ref[...] = jnp.zeros_like(bs_ref)

        gs_ref[...] += jnp.sum(dh * xhat_, axis=0, keepdims=True)
        bs_ref[...] += jnp.sum(dh, axis=0, keepdims=True)

    row = pl.BlockSpec((tm, D), lambda i: (i, 0))
    vec = pl.BlockSpec((1, D), lambda i: (0, 0))
    col = pl.BlockSpec((tm, 1), lambda i: (i, 0))
    return pl.pallas_call(
        body, name=name, grid=(T // tm,), in_specs=[row, row, row, col, vec], out_specs=[row, row, vec, vec],
        out_shape=[jax.ShapeDtypeStruct((T, D), F32), jax.ShapeDtypeStruct((T, D), BF16),
                   jax.ShapeDtypeStruct((1, D), F32), jax.ShapeDtypeStruct((1, D), F32)],
        compiler_params=_params(),
    )(dr_next, dh_branch, xhat, rstd, g)


def _ln3_loss(hprev, f, target, mask, g, b, name):
    T, D = hprev.shape
    tm = _rows_tile(T, D)

    def body(hp_ref, f_ref, t_ref, m_ref, g_ref, b_ref, dr_ref, drb_ref, gs_ref, bs_ref, ls_ref):
        i = pl.program_id(0)
        r = ALPHA * hp_ref[...] + 0.5 * f_ref[...]
        xhat, rstd = _ln_stats(r)
        g_ = g_ref[...]
        y = xhat * g_ + b_ref[...]
        err = (y - t_ref[...]) * m_ref[...]
        dy = err * (1.0 / D)
        dr = _ln_bwd_math(dy, xhat, rstd, g_)
        dr_ref[...] = dr
        drb_ref[...] = (0.5 * dr).astype(BF16)

        @pl.when(i == 0)
        def _():
            gs_ref[...] = jnp.zeros_like(gs_ref)
            bs_ref[...] = jnp.zeros_like(bs_ref)
            ls_ref[...] = jnp.zeros_like(ls_ref)

        gs_ref[...] += jnp.sum(dy * xhat, axis=0, keepdims=True)
        bs_ref[...] += jnp.sum(dy, axis=0, keepdims=True)
        ls_ref[...] += jnp.sum(err * err, axis=0, keepdims=True)

    row = pl.BlockSpec((tm, D), lambda i: (i, 0))
    vec = pl.BlockSpec((1, D), lambda i: (0, 0))
    col = pl.BlockSpec((tm, 1), lambda i: (i, 0))
    return pl.pallas_call(
        body, name=name, grid=(T // tm,), in_specs=[row, row, row, col, vec, vec],
        out_specs=[row, row, vec, vec, vec],
        out_shape=[jax.ShapeDtypeStruct((T, D), F32), jax.ShapeDtypeStruct((T, D), BF16),
                   jax.ShapeDtypeStruct((1, D), F32), jax.ShapeDtypeStruct((1, D), F32),
                   jax.ShapeDtypeStruct((1, D), F32)],
        compiler_params=_params(),
    )(hprev, f, target, mask, g, b)


def _down(x, k):
    rows = lax.broadcasted_iota(jnp.int32, x.shape, 0)
    return jnp.where(rows >= k, pltpu.roll(x, k, axis=0), 0.0)


def _up(x, k):
    n = x.shape[0]
    rows = lax.broadcasted_iota(jnp.int32, x.shape, 0)
    return jnp.where(rows < n - k, pltpu.roll(x, n - k, axis=0), 0.0)


def _conv_fwd(u, conv_w, L, CC, name):
    T = u.shape[0]
    tc = _tile(CC, (256, 128))
    n = CC // tc

    def body(b_ref, c_ref, x_ref, w_ref, o_ref):
        cx = c_ref[...] * x_ref[...]
        w = w_ref[...]
        conv = w[0:1] * _down(cx, 2) + w[1:2] * _down(cx, 1) + w[2:3] * cx
        o_ref[...] = (b_ref[...] * conv).astype(BF16)

    blk = lambda off: pl.BlockSpec((L, tc), lambda s, j: (s, j + off * n))
    return pl.pallas_call(
        body, name=name, grid=(T // L, n),
        in_specs=[blk(0), blk(1), blk(2), pl.BlockSpec((3, tc), lambda s, j: (0, j))],
        out_specs=pl.BlockSpec((L, tc), lambda s, j: (s, j)),
        out_shape=jax.ShapeDtypeStruct((T, CC), BF16), compiler_params=_params(),
    )(u, u, u, conv_w)


def _conv_bwd(u, dy, conv_w, L, CC, name):
    T = u.shape[0]
    tc = _tile(CC, (256, 128))
    n = CC // tc

    def body(b_ref, c_ref, x_ref, dy_ref, w_ref, db_ref, dc_ref, dx_ref, dw_ref):
        s = pl.program_id(1)
        c_, x_ = c_ref[...], x_ref[...]
        cx = c_ * x_
        w = w_ref[...]
        cx1, cx2 = _down(cx, 1), _down(cx, 2)
        conv = w[0:1] * cx2 + w[1:2] * cx1 + w[2:3] * cx
        dy_ = dy_ref[...]
        db_ref[...] = (dy_ * conv).astype(BF16)
        dconv = dy_ * b_ref[...]
        dcx = w[2:3] * dconv + w[1:2] * _up(dconv, 1) + w[0:1] * _up(dconv, 2)
        dc_ref[...] = (dcx * x_).astype(BF16)
        dx_ref[...] = (dcx * c_).astype(BF16)

        @pl.when(s == 0)
        def _():
            dw_ref[...] = jnp.zeros_like(dw_ref)

        dw_ref[0:1, :] += jnp.sum(dconv * cx2, axis=0, keepdims=True)
        dw_ref[1:2, :] += jnp.sum(dconv * cx1, axis=0, keepdims=True)
        dw_ref[2:3, :] += jnp.sum(dconv * cx, axis=0, keepdims=True)

    blk = lambda off: pl.BlockSpec((L, tc), lambda j, s: (s, j + off * n))
    out = pl.BlockSpec((L, tc), lambda j, s: (s, j))
    act = jax.ShapeDtypeStruct((T, CC), BF16)
    return pl.pallas_call(
        body, name=name, grid=(n, T // L),
        in_specs=[blk(0), blk(1), blk(2), blk(0), pl.BlockSpec((3, tc), lambda j, s: (0, j))],
        out_specs=[out, out, out, pl.BlockSpec((3, tc), lambda j, s: (0, j))],
        out_shape=[act, act, act, jax.ShapeDtypeStruct((3, CC), F32)], compiler_params=_params(),
    )(u, u, u, dy, conv_w)


def _pool_counts(shape, g):
    rows = lax.broadcasted_iota(jnp.int32, shape, 0)
    win = jnp.left_shift(jnp.int32(POOL_WINDOWS[0]), g)
    return jnp.minimum(rows + 1, win).astype(F32)


def _pick_group(g, vals):
    out = vals[-1]
    for i in range(len(vals) - 2, -1, -1):
        out = jnp.where(g == i, vals[i], out)
    return out


def _pool_fwd(u, pool_w, pool_scale, L, CC, PG, name):
    T = u.shape[0]
    G = len(POOL_WINDOWS)
    off = 3 * CC // PG

    def body(z_ref, w_ref, sc_ref, y_ref, d_ref):
        g = pl.program_id(1)
        z = z_ref[...]
        s1 = z + _down(z, 1)
        s2 = s1 + _down(s1, 2)
        s3 = s2 + _down(s2, 4)
        s4 = s3 + _down(s3, 8)
        pooled = _pick_group(g, [s1, s2, s3, s4]) / _pool_counts(z.shape, g)
        d = (pooled - z).astype(BF16)
        d_ref[...] = d
        q = jnp.dot(d, w_ref[...], preferred_element_type=F32)
        y_ref[...] = (q * sc_ref[...]).astype(BF16)

    blk = pl.BlockSpec((L, PG), lambda s, g: (s, g))
    act = jax.ShapeDtypeStruct((T, G * PG), BF16)
    return pl.pallas_call(
        body, name=name, grid=(T // L, G),
        in_specs=[pl.BlockSpec((L, PG), lambda s, g: (s, off + g)), pl.BlockSpec((None, PG, PG), lambda s, g: (g, 0, 0)),
                  pl.BlockSpec((1, PG), lambda s, g: (0, g))],
        out_specs=[blk, blk], out_shape=[act, act], compiler_params=_params(),
    )(u, pool_w, pool_scale)


def _pool_bwd(d, dy, pool_w, pool_scale, L, CC, PG, name):
    T = d.shape[0]
    G = len(POOL_WINDOWS)
    off = CC // PG
    PGs = PG // 4

    def body(d_ref, dy_ref, w_ref, sc_ref, dz_ref, dw_ref, dsc_ref):
        g = pl.program_id(0)
        s = pl.program_id(1)
        d_ = d_ref[...]
        w = w_ref[...]
        dy_ = dy_ref[...]
        q = jnp.dot(d_, w, preferred_element_type=F32)
        dq = (dy_ * sc_ref[...]).astype(BF16)
        dd = lax.dot_general(dq, w, (((1,), (1,)), ((), ())), preferred_element_type=F32)
        e = dd / _pool_counts(dd.shape, g)
        a1 = e + _up(e, 1)
        a2 = a1 + _up(a1, 2)
        a3 = a2 + _up(a2, 4)
        a4 = a3 + _up(a3, 8)
        dz_ref[...] = (_pick_group(g, [a1, a2, a3, a4]) - dd).astype(BF16)

        @pl.when(s == 0)
        def _():
            dw_ref[...] = jnp.zeros_like(dw_ref)
            dsc_ref[...] = jnp.zeros_like(dsc_ref)

        dw = lax.dot_general(d_, dq, (((0,), (0,)), ((), ())), preferred_element_type=F32)
        for k in range(4):
            dw_ref[k] += dw[k * PGs:(k + 1) * PGs, :]
        dsc_ref[...] += jnp.sum(dy_ * q, axis=0, keepdims=True)

    return pl.pallas_call(
        body, name=name, grid=(G, T // L),
        in_specs=[pl.BlockSpec((L, PG), lambda g, s: (s, g)), pl.BlockSpec((L, PG), lambda g, s: (s, off + g)),
                  pl.BlockSpec((None, PG, PG), lambda g, s: (g, 0, 0)), pl.BlockSpec((1, PG), lambda g, s: (0, g))],
        out_specs=[pl.BlockSpec((L, PG), lambda g, s: (s, g)),
                   pl.BlockSpec((4, None, PGs, PG), lambda g, s: (0, g, 0, 0)),
                   pl.BlockSpec((1, PG), lambda g, s: (0, g))],
        out_shape=[jax.ShapeDtypeStruct((T, G * PG), BF16), jax.ShapeDtypeStruct((4, G, PGs, PG), F32),
                   jax.ShapeDtypeStruct((1, G * PG), F32)],
        compiler_params=_params(),
    )(d, dy, pool_w, pool_scale)


def _input_grad(dr1, dh_branch, L, name):
    T, D = dr1.shape
    nseq = T // L
    tc = _tile(D, (512, 256, 128))

    def body(a_ref, b_ref, gx_ref, gm_ref):
        dh = ALPHA * a_ref[...] + b_ref[...]
        gm_ref[...] = dh[:N_META]
        gx_ref[...] = dh[N_META:]

    blk = pl.BlockSpec((L, tc), lambda s, j: (s, j))
    return pl.pallas_call(
        body, name=name, grid=(nseq, D // tc), in_specs=[blk, blk],
        out_specs=[pl.BlockSpec((None, L - N_META, tc), lambda s, j: (s, 0, j)),
                   pl.BlockSpec((None, N_META, tc), lambda s, j: (s, 0, j))],
        out_shape=[jax.ShapeDtypeStruct((nseq, L - N_META, D), F32), jax.ShapeDtypeStruct((nseq, N_META, D), F32)],
        compiler_params=_params(),
    )(dr1, dh_branch)


def _chip():
    return 2 * lax.axis_index("x") + lax.axis_index("y")


def _core():
    return lax.axis_index("c")


def _cast_into_full(w, sh, name):
    tr = _rows_tile(sh.srows, sh.scols)
    nb = sh.srows // tr

    def body(w_ref, o_ref):
        o_ref[...] = w_ref[...].astype(BF16)

    if sh.axis == 0:
        out = pl.BlockSpec((tr, sh.scols), lambda i: (_chip() * nb + i, 0))
    else:
        out = pl.BlockSpec((tr, sh.scols), lambda i: (i, _chip()))
    return pl.pallas_call(
        body, name=name, grid=(nb,), in_specs=[pl.BlockSpec((tr, sh.scols), lambda i: (i, 0))], out_specs=out,
        out_shape=jax.ShapeDtypeStruct((sh.rows, sh.cols), BF16), compiler_params=_params(),
    )(w)


def _add_halves(dw, theirs, sh, name):
    R, C = sh.half_all_shape
    tr = _rows_tile(sh.h, C)
    nh, ns = sh.h // tr, sh.srows // tr

    def body(a_ref, b_ref, o_ref):
        o_ref[...] = (a_ref[...].astype(F32) + b_ref[...].astype(F32)).astype(BF16)

    blk = pl.BlockSpec((tr, C), lambda i: (i, 0))
    mine = pl.BlockSpec((tr, C), lambda i: ((i // nh) * ns + _core() * nh + i % nh, 0))
    return pl.pallas_call(body, name=name, grid=(R // tr,), in_specs=[mine, blk], out_specs=blk,
                          out_shape=jax.ShapeDtypeStruct((R, C), BF16), compiler_params=_params())(dw, theirs)


def _sum_parts(part, recv, sh, name):
    h, C = sh.half_shape
    tr = _rows_tile(h, C)
    nh = h // tr

    def body(o_ref, r_ref, out_ref):
        acc = o_ref[...].astype(F32)
        for k in range(3):
            acc = acc + r_ref[k].astype(F32)
        out_ref[...] = acc

    if sh.axis == 0:
        own = pl.BlockSpec((tr, C), lambda i: (_chip() * nh + i, 0))
    else:
        own = pl.BlockSpec((tr, C), lambda i: (i, _chip()))
    return pl.pallas_call(
        body, name=name, grid=(nh,), in_specs=[own, pl.BlockSpec((3, tr, C), lambda i: (0, i, 0))],
        out_specs=pl.BlockSpec((tr, C), lambda i: (_core() * nh + i, 0)),
        out_shape=jax.ShapeDtypeStruct((sh.srows, sh.scols), F32), compiler_params=_params(),
    )(part, recv)


def _adamw(w, g, m, v, name):
    R, C = w.shape
    tr = _rows_tile(R, C, mult=8)

    def body(w_ref, g_ref, m_ref, v_ref, d_ref, nm_ref, nv_ref):
        g_ = g_ref[...]
        m_ = ADAM_B1 * m_ref[...] + (1.0 - ADAM_B1) * g_
        v_ = ADAM_B2 * v_ref[...] + (1.0 - ADAM_B2) * (g_ * g_)
        m_hat = m_ / (1.0 - ADAM_B1 ** ADAM_STEP)
        v_hat = v_ / (1.0 - ADAM_B2 ** ADAM_STEP)
        d_ref[...] = -ADAM_LR * (m_hat / (jnp.sqrt(v_hat) + ADAM_EPS) + ADAM_WD * w_ref[...])
        nm_ref[...] = m_
        nv_ref[...] = v_

    blk = pl.BlockSpec((tr, C), lambda i: (i, 0))
    shp = jax.ShapeDtypeStruct((R, C), F32)
    return pl.pallas_call(body, name=name, grid=(R // tr,), in_specs=[blk] * 4, out_specs=[blk] * 3,
                          out_shape=[shp] * 3, compiler_params=_params())(w, g, m, v)


def _pos():
    return lax.axis_index("x"), lax.axis_index("y"), lax.axis_index("c")


def _other_chips(x, y):
    return [(1 - x, y), (x, 1 - y), (1 - x, 1 - y)]


class _Sharded:
    def __init__(self, full_shape, axis):
        self.rows, self.cols = full_shape
        self.axis = axis
        self.srows = self.rows // 4 if axis == 0 else self.rows
        self.scols = self.cols if axis == 0 else self.cols // 4
        assert self.srows % 2 == 0
        self.h = self.srows // 2
        self.half_all_shape = (4 * self.h, self.cols) if axis == 0 else (self.h, self.cols)
        self.half_shape = (self.h, self.scols)

    def shard(self, ref, q):
        if self.axis == 0:
            return ref.at[pl.ds(q * self.srows, self.srows), :]
        return ref.at[:, pl.ds(pl.multiple_of(q * self.scols, LANES), self.scols)]

    def half(self, ref, q, c):
        if self.axis == 0:
            return ref.at[pl.ds(q * self.srows + c * self.h, self.h), :]
        return ref.at[pl.ds(c * self.h, self.h), pl.ds(pl.multiple_of(q * self.scols, LANES), self.scols)]

    def half_in_stack(self, ref, q):
        if self.axis == 0:
            return ref.at[pl.ds(q * self.h, self.h), :]
        return ref.at[:, pl.ds(pl.multiple_of(q * self.scols, LANES), self.scols)]


ANY = pl.BlockSpec(memory_space=pl.ANY)
DMA_ROW_ALIGN = 16
CHUNK_COUNTS = (16, 8, 43, 4, 2)


def _row_chunks(rows):
    n = 1
    if rows % DMA_ROW_ALIGN == 0:
        n = _tile(rows // DMA_ROW_ALIGN, CHUNK_COUNTS)
        n = n if n in CHUNK_COUNTS else 1
    size = rows // n
    return [(k * size, size) for k in range(n)]


class _Pieces:
    def __init__(self, src, dst, send_sem, recv_sem, to):
        self.args = (src, dst, send_sem, recv_sem, to)

    def _copy(self, rows=None):
        src, dst, send_sem, recv_sem, to = self.args
        if rows is not None:
            src, dst = src.at[pl.ds(*rows), :], dst.at[pl.ds(*rows), :]
        return pltpu.make_async_remote_copy(src_ref=src, dst_ref=dst, send_sem=send_sem, recv_sem=recv_sem,
                                            device_id=to, device_id_type=MESH)

    def start(self):
        for rows in _row_chunks(self.args[0].shape[0]):
            self._copy(rows).start()

    def wait_send(self):
        self._copy().wait_send()

    def wait_recv(self):
        self._copy().wait_recv()

    def wait(self):
        self._copy().wait()


HBM = pl.BlockSpec(memory_space=pltpu.HBM)
SEM = pl.BlockSpec(memory_space=pltpu.SEMAPHORE)
SPLIT_COPY = pltpu.CompilerParams(has_side_effects=pltpu.SideEffectType.DATAFLOW_SIDE_EFFECTING)
TOKEN = jax.ShapeDtypeStruct((8, LANES), F32)


def _in_hbm(a):
    return pltpu.with_memory_space_constraint(a, pltpu.HBM)


def _gather_start(full, sh, name):
    def body(full_ref, send_sems, recv_sems, thru_ref, token):
        x, y, c = _pos()
        mine = sh.half(full_ref, 2 * x + y, c)
        for j, (cx, cy) in enumerate(_other_chips(x, y)):
            _Pieces(mine, mine, send_sems.at[j], recv_sems.at[j], (cx, cy, c)).start()
        token[...] = jnp.zeros_like(token)

    return pl.pallas_call(
        body, name=name, in_specs=(HBM,),
        out_shape=(pltpu.SemaphoreType.DMA((3,)), pltpu.SemaphoreType.DMA((3,)), pltpu.HBM(full.shape, full.dtype), TOKEN),
        out_specs=(SEM, SEM, HBM, pl.BlockSpec(memory_space=pltpu.VMEM)), input_output_aliases={0: 2},
        compiler_params=SPLIT_COPY,
    )(_in_hbm(full))


def _gather_wait(send_sems, recv_sems, full, after, sh, name):
    def body(full_ref, send_sems, recv_sems, after_ref, out_ref):
        x, y, c = _pos()
        mine = sh.half(full_ref, 2 * x + y, c)
        for j, (cx, cy) in enumerate(_other_chips(x, y)):
            cp = pltpu.make_async_remote_copy(src_ref=mine, dst_ref=sh.half(full_ref, 2 * cx + cy, c),
                                              send_sem=send_sems.at[j], recv_sem=recv_sems.at[j],
                                              device_id=(cx, cy, c), device_id_type=MESH)
            cp.wait_send()
            cp.wait_recv()

    return pl.pallas_call(
        body, name=name, in_specs=(HBM, SEM, SEM, ANY), out_shape=pltpu.HBM(full.shape, full.dtype), out_specs=HBM,
        input_output_aliases={0: 0}, compiler_params=SPLIT_COPY,
    )(full, send_sems, recv_sems, after)


def _gather_forward(full, sh, name):
    def body(full_ref, out_ref, send_sems, recv_sems):
        x, y, c = _pos()
        sib = (x, y, 1 - c)
        chips = _other_chips(x, y)
        passed = []
        for j, (cx, cy) in enumerate(chips):
            landed_src = sh.half(full_ref, 2 * cx + cy, c)
            landed_dst = sh.half(out_ref, 2 * cx + cy, c)
            cp = _Pieces(landed_src, landed_dst, send_sems.at[j], recv_sems.at[j], sib)
            cp.start()
            passed.append(cp)
        for j, (cx, cy) in enumerate(chips):
            theirs = sh.half(out_ref, 2 * cx + cy, 1 - c)
            _Pieces(theirs, theirs, send_sems.at[j], recv_sems.at[j], sib).wait_recv()
        for cp in passed:
            cp.wait_send()

    return pl.pallas_call(
        body, name=name, in_specs=[ANY], out_specs=ANY, out_shape=jax.ShapeDtypeStruct(full.shape, full.dtype),
        input_output_aliases={0: 0},
        scratch_shapes=[pltpu.SemaphoreType.DMA((3,)), pltpu.SemaphoreType.DMA((3,))],
    )(full)


def _rs_sibling(dw, sh, name):
    n = 4 if sh.axis == 0 else 1

    def body(dw_ref, theirs_ref, send_sems, recv_sems):
        x, y, c = _pos()
        sib = (x, y, 1 - c)
        cps = []
        for k in range(n):
            if sh.axis == 0:
                give = dw_ref.at[pl.ds(k * sh.srows + (1 - c) * sh.h, sh.h), :]
                theirs_k = theirs_ref.at[pl.ds(k * sh.h, sh.h), :]
            else:
                give = dw_ref.at[pl.ds((1 - c) * sh.h, sh.h), :]
                theirs_k = theirs_ref
            cps.append(_Pieces(give, theirs_k, send_sems.at[k], recv_sems.at[k], sib))
        for cp in cps:
            cp.start()
        for cp in cps:
            cp.wait()

    return pl.pallas_call(
        body, name=name, in_specs=[ANY], out_specs=ANY, out_shape=jax.ShapeDtypeStruct(sh.half_all_shape, dw.dtype),
        scratch_shapes=[pltpu.SemaphoreType.DMA((n,)), pltpu.SemaphoreType.DMA((n,))],
    )(dw)


def _rs_start(part, sh, name):
    land_shape = (3,) + sh.half_shape

    def body(p_ref, land_ref, send_sems, recv_sems, p_thru, land_thru, token):
        x, y, c = _pos()
        for j, (cx, cy) in enumerate(_other_chips(x, y)):
            _Pieces(sh.half_in_stack(p_ref, 2 * cx + cy), land_ref.at[j], send_sems.at[j], recv_sems.at[j],
                    (cx, cy, c)).start()
        token[...] = jnp.zeros_like(token)

    return pl.pallas_call(
        body, name=name, in_specs=(HBM, HBM),
        out_shape=(pltpu.SemaphoreType.DMA((3,)), pltpu.SemaphoreType.DMA((3,)), pltpu.HBM(part.shape, part.dtype),
                   pltpu.HBM(land_shape, part.dtype), TOKEN),
        out_specs=(SEM, SEM, HBM, HBM, pl.BlockSpec(memory_space=pltpu.VMEM)), input_output_aliases={0: 2, 1: 3},
        compiler_params=SPLIT_COPY,
    )(_in_hbm(part), _in_hbm(lax.empty(land_shape, part.dtype)))


def _rs_wait(send_sems, recv_sems, part, land, after, sh, name):
    def body(p_ref, land_ref, send_sems, recv_sems, after_ref, p_out, land_out):
        x, y, c = _pos()
        for j, (cx, cy) in enumerate(_other_chips(x, y)):
            cp = pltpu.make_async_remote_copy(src_ref=sh.half_in_stack(p_ref, 2 * cx + cy), dst_ref=land_ref.at[j],
                                              send_sem=send_sems.at[j], recv_sem=recv_sems.at[j],
                                              device_id=(cx, cy, c), device_id_type=MESH)
            cp.wait_send()
            cp.wait_recv()

    return pl.pallas_call(
        body, name=name, in_specs=(HBM, HBM, SEM, SEM, ANY),
        out_shape=(pltpu.HBM(part.shape, part.dtype), pltpu.HBM(land.shape, land.dtype)), out_specs=(HBM, HBM),
        input_output_aliases={0: 0, 1: 1}, compiler_params=SPLIT_COPY,
    )(part, land, send_sems, recv_sems, after)


def _share_halves(g, sh, name):
    def body(g_ref, out_ref, send_sem, recv_sem):
        x, y, c = _pos()
        cp = _Pieces(g_ref.at[pl.ds(c * sh.h, sh.h), :], out_ref.at[pl.ds(c * sh.h, sh.h), :], send_sem, recv_sem,
                     (x, y, 1 - c))
        cp.start()
        theirs = out_ref.at[pl.ds((1 - c) * sh.h, sh.h), :]
        _Pieces(theirs, theirs, send_sem, recv_sem, (x, y, 1 - c)).wait_recv()
        cp.wait_send()

    return pl.pallas_call(
        body, name=name, in_specs=[ANY], out_specs=ANY, out_shape=jax.ShapeDtypeStruct(g.shape, g.dtype),
        input_output_aliases={0: 0}, scratch_shapes=[pltpu.SemaphoreType.DMA, pltpu.SemaphoreType.DMA],
    )(g)


def _exchange_small(v, reduce, name):
    R, C = v.shape

    def body(v_ref, o_ref, buf, send_sems, recv_sems):
        x, y, c = _pos()
        me = 4 * x + 2 * y + c
        slots = buf if reduce else o_ref
        slots[me] = v_ref[...]
        cps = []
        for k in range(1, 8):
            px = 1 - x if k & 4 else x
            py = 1 - y if k & 2 else y
            pc = 1 - c if k & 1 else c
            cps.append((_Pieces(v_ref, slots.at[me], send_sems.at[k - 1], recv_sems.at[k - 1], (px, py, pc)),
                        4 * px + 2 * py + pc))
        for cp, _ in cps:
            cp.start()
        for k, (cp, peer) in enumerate(cps):
            pltpu.make_async_remote_copy(
                src_ref=v_ref, dst_ref=slots.at[peer], send_sem=send_sems.at[k], recv_sem=recv_sems.at[k],
                device_id=(x, y, c), device_id_type=MESH).wait_recv()
        if reduce:
            acc = buf[0]
            for d in range(1, 8):
                acc = acc + buf[d]
            o_ref[...] = acc
        for cp, _ in cps:
            cp.wait_send()

    vm = pl.BlockSpec(memory_space=pltpu.VMEM)
    out_shape = jax.ShapeDtypeStruct((R, C) if reduce else (8, R, C), v.dtype)
    scratch = [pltpu.VMEM((8, R, C) if reduce else (8, LANES), v.dtype),
               pltpu.SemaphoreType.DMA((7,)), pltpu.SemaphoreType.DMA((7,))]
    return pl.pallas_call(body, name=name, in_specs=[vm], out_specs=vm, out_shape=out_shape, scratch_shapes=scratch,
                          compiler_params=_params())(v)


def _pack_rows(parts, width):
    rows, offs, at = [], [], 0
    for p in parts:
        flat = p.reshape(-1)
        n = 8 * -(-flat.shape[0] // (8 * width))
        flat = jnp.pad(flat, (0, n * width - flat.shape[0]))
        rows.append(flat.reshape(n, width))
        offs.append(at)
        at += n
    return jnp.concatenate(rows, axis=0), offs


def _unpack_rows(packed, offs, shapes):
    out = []
    for off, shp in zip(offs, shapes):
        size = 1
        for s in shp:
            size *= s
        n = -(-size // packed.shape[1])
        out.append(packed[off:off + n].reshape(-1)[:size].reshape(shp))
    return out


def kernel(x, meta_tokens, ffn1_w_gu, ffn1_w_down, ln1_g, ln1_b, w_in, conv_w, pool_w, pool_scale, w_out, ln2_g, ln2_b, ffn2_w_gu, ffn2_w_down, ln3_g, ln3_b, loss_target, m_meta_tokens, m_ffn1_w_gu, m_ffn1_w_down, m_ln1_g, m_ln1_b, m_w_in, m_conv_w, m_pool_w, m_pool_scale, m_w_out, m_ln2_g, m_ln2_b, m_ffn2_w_gu, m_ffn2_w_down, m_ln3_g, m_ln3_b, v_meta_tokens, v_ffn1_w_gu, v_ffn1_w_down, v_ln1_g, v_ln1_b, v_w_in, v_conv_w, v_pool_w, v_pool_scale, v_w_out, v_ln2_g, v_ln2_b, v_ffn2_w_gu, v_ffn2_w_down, v_ln3_g, v_ln3_b):
    nseq, S, D = x.shape
    L = S + N_META
    T = nseq * L
    F = ffn1_w_down.shape[1] * 4
    CC = conv_w.shape[2] * 4
    G, PGs, PG = pool_w.shape[1:]
    PC = G * PG
    IN = w_in.shape[2] * 4
    qx, qy, qc = _pos()
    q = 2 * qx + qy

    sh_gu = _Sharded((D, 2 * F), 1)
    sh_down = _Sharded((F, D), 0)
    sh_in = _Sharded((D, IN), 1)
    sh_out = _Sharded((CC + PC, D), 0)
    sh_pool = _Sharded((4 * G * PGs, PG), 0)

    gathers = {}
    tok = jnp.zeros((), F32)
    for tag, w, sh in (("gu1", ffn1_w_gu, sh_gu), ("d1", ffn1_w_down, sh_down), ("in", w_in, sh_in),
                       ("out", w_out, sh_out), ("gu2", ffn2_w_gu, sh_gu), ("d2", ffn2_w_down, sh_down)):
        full = _cast_into_full(w[0], sh, "cast_" + tag)
        s_, r_, thru, t_ = _gather_start(full, sh, "ag_start_" + tag)
        gathers[tag] = (s_, r_, thru, sh)
        tok = tok + t_[0, 0]

    def arrive(tag, after):
        s_, r_, thru, sh = gathers.pop(tag)
        full = _gather_wait(s_, r_, thru, after, sh, "ag_wait_" + tag)
        return _gather_forward(full, sh, "ag_fwd_" + tag)

    small_w = [meta_tokens, conv_w[0], pool_w[0]]
    packed, offs = _pack_rows(small_w, 4 * LANES)
    slots = _exchange_small(packed, False, "ag_small")
    per_chip = [_unpack_rows(slots[2 * k], offs, [p.shape for p in small_w]) for k in range(4)]
    meta_full = jnp.concatenate([p[0] for p in per_chip], axis=1)
    conv_full = jnp.concatenate([p[1] for p in per_chip], axis=1)
    poolw_full = jnp.concatenate([p[2] for p in per_chip], axis=1).astype(BF16)
    pscale = pool_scale

    h0 = jnp.concatenate([jnp.broadcast_to(meta_full[None], (nseq, N_META, D)), x], axis=1).reshape(T, D)
    h0b = (h0 + tok).astype(BF16)
    tgt = jnp.pad(loss_target, ((0, 0), (N_META, 0), (0, 0))).reshape(T, D)
    mask = (lax.broadcasted_iota(jnp.int32, (nseq, L, 1), 1) >= N_META).astype(F32).reshape(T, 1)

    wgu1 = arrive("gu1", h0b)
    gu1 = _mm_nn(h0b, wgu1, BF16, "ffn_gu")
    wd1 = arrive("d1", gu1)
    a1 = _silu_mul(gu1, "silu_mul")
    f1 = _mm_nn(a1, wd1, F32, "ffn_down")
    win = arrive("in", f1)
    h1, h1b, xh1, rs1 = _ln_fwd(h0, f1, ln1_g, ln1_b, 0.5, "ln_fwd")

    u = _mm_nn(h1b, win, F32, "mix_in")
    wout = arrive("out", u)
    yc = _conv_fwd(u, conv_full, L, CC, "conv_fwd")
    yp, dpool = _pool_fwd(u, poolw_full, pscale, L, CC, PG, "pool_fwd")
    ymix = jnp.concatenate([yc, yp], axis=1)
    o = _mm_nn(ymix, wout, F32, "mix_out")
    wgu2 = arrive("gu2", o)
    h2, h2b, xh2, rs2 = _ln_fwd(h1, o, ln2_g, ln2_b, 1.0, "ln_fwd")

    gu2 = _mm_nn(h2b, wgu2, BF16, "ffn_gu")
    wd2 = arrive("d2", gu2)
    a2 = _silu_mul(gu2, "silu_mul")
    f2 = _mm_nn(a2, wd2, F32, "ffn_down")
    dr3, dr3b, gs3, bs3, lsum = _ln3_loss(h2, f2, tgt, mask, ln3_g, ln3_b, "ln3_loss")
    loss = lax.psum(0.5 * jnp.sum(lsum) / D, ALL_AXES)

    pending = {}

    def rs_begin(dw, sh, tag):
        theirs = _rs_sibling(dw, sh, "rs_sib_" + tag)
        part = _add_halves(dw, theirs, sh, "rs_add_" + tag)
        s_, r_, p_thru, land, t_ = _rs_start(part, sh, "rs_start_" + tag)
        pending[tag] = (s_, r_, p_thru, land, sh)
        return t_

    def rs_end(tag, after):
        s_, r_, p_thru, land, sh = pending.pop(tag)
        part, recv = _rs_wait(s_, r_, p_thru, land, after, sh, "rs_wait_" + tag)
        g = _sum_parts(part, recv, sh, "rs_sum_" + tag)
        return _share_halves(g, sh, "rs_share_" + tag)

    def ffn_bwd(drb, a, gu, hb, wgu, wd, sfx, after):
        da = _mm_nt(drb, wd, BF16, "ffn_da", after=after)
        dwd = _mm_tn(a, drb, "ffn_dwd")
        t_d = rs_begin(dwd, sh_down, "d" + sfx)
        dgu = _swiglu_bwd(da, gu, "swiglu_bwd")
        dwgu = _mm_tn(hb, dgu, "ffn_dwgu", after=t_d)
        t_gu = rs_begin(dwgu, sh_gu, "gu" + sfx)
        return _mm_nt(dgu, wgu, F32, "ffn_dh", after=t_gu)

    dh2 = ffn_bwd(dr3b, a2, gu2, h2b, wgu2, wd2, "2", None)

    dr2, dr2b, gs2, bs2 = _ln_bwd(dr3, dh2, xh2, rs2, ln2_g, 1.0, "ln_bwd")
    dymix = _mm_nt(dr2b, wout, F32, "mix_dy")
    dwout = _mm_tn(ymix, dr2b, "mix_dwout")
    t_out = rs_begin(dwout, sh_out, "out")
    db_, dc_, dx_, dconvw = _conv_bwd(u, dymix, conv_full, L, CC, "conv_bwd")
    dz_, dpoolw, dpscale = _pool_bwd(dpool, dymix, poolw_full, pscale, L, CC, PG, "pool_bwd")
    rs_begin(dpoolw.reshape(4 * G * PGs, PG).astype(BF16), sh_pool, "pool")
    du = jnp.concatenate([db_, dc_, dx_, dz_], axis=1)
    dwin = _mm_tn(h1b, du, "mix_dwin", after=t_out)
    t_in = rs_begin(dwin, sh_in, "in")
    dh1 = _mm_nt(du, win, F32, "mix_dh", after=t_in)

    dr1, dr1b, gs1, bs1 = _ln_bwd(dr2, dh1, xh1, rs1, ln1_g, 0.5, "ln_bwd")
    dh0f = ffn_bwd(dr1b, a1, gu1, h0b, wgu1, wd1, "1", None)
    grad_x, gmeta = _input_grad(dr1, dh0f, L, "input_grad")

    g_d2 = rs_end("d2", grad_x)
    g_gu2 = rs_end("gu2", g_d2)
    g_out = rs_end("out", g_gu2)
    g_in = rs_end("in", g_out)
    g_pool = rs_end("pool", g_in).reshape(G, PGs, PG)
    g_d1 = rs_end("d1", g_pool)
    g_gu1 = rs_end("gu1", g_d1)

    small_g = [gs1, bs1, gs2, bs2, gs3, bs3, dpscale, dconvw, jnp.sum(gmeta, axis=0)]
    gpacked, goffs = _pack_rows(small_g, D)
    gsum = _exchange_small(gpacked, True, "ar_small")
    (g_ln1g, g_ln1b, g_ln2g, g_ln2b, g_ln3g, g_ln3b, g_pscale, g_conv_all, g_meta_all) = _unpack_rows(
        gsum, goffs, [p.shape for p in small_g])
    g_meta = lax.dynamic_slice_in_dim(g_meta_all, q * (D // 4), D // 4, axis=1)
    g_conv = lax.dynamic_slice_in_dim(g_conv_all, q * (CC // 4), CC // 4, axis=1)

    names = ["meta_tokens", "ffn1_w_gu", "ffn1_w_down", "ln1_g", "ln1_b", "w_in", "conv_w", "pool_w", "pool_scale",
             "w_out", "ln2_g", "ln2_b", "ffn2_w_gu", "ffn2_w_down", "ln3_g", "ln3_b"]
    ws = dict(zip(names, [meta_tokens, ffn1_w_gu, ffn1_w_down, ln1_g, ln1_b, w_in, conv_w, pool_w, pool_scale, w_out,
                          ln2_g, ln2_b, ffn2_w_gu, ffn2_w_down, ln3_g, ln3_b]))
    ms = dict(zip(names, [m_meta_tokens, m_ffn1_w_gu, m_ffn1_w_down, m_ln1_g, m_ln1_b, m_w_in, m_conv_w, m_pool_w,
                          m_pool_scale, m_w_out, m_ln2_g, m_ln2_b, m_ffn2_w_gu, m_ffn2_w_down, m_ln3_g, m_ln3_b]))
    vs = dict(zip(names, [v_meta_tokens, v_ffn1_w_gu, v_ffn1_w_down, v_ln1_g, v_ln1_b, v_w_in, v_conv_w, v_pool_w,
                          v_pool_scale, v_w_out, v_ln2_g, v_ln2_b, v_ffn2_w_gu, v_ffn2_w_down, v_ln3_g, v_ln3_b]))
    grads = {
        "meta_tokens": g_meta, "ffn1_w_gu": g_gu1[None], "ffn1_w_down": g_d1[None], "ln1_g": g_ln1g, "ln1_b": g_ln1b,
        "w_in": g_in[None], "conv_w": g_conv[None], "pool_w": g_pool[None], "pool_scale": g_pscale,
        "w_out": g_out[None], "ln2_g": g_ln2g, "ln2_b": g_ln2b, "ffn2_w_gu": g_gu2[None], "ffn2_w_down": g_d2[None],
        "ln3_g": g_ln3g, "ln3_b": g_ln3b,
    }
    big = ("ffn1_w_gu", "ffn1_w_down", "w_in", "w_out", "ffn2_w_gu", "ffn2_w_down")
    delta, new_m, new_v = {}, {}, {}
    for n in big:
        d_, m_, v_ = _adamw(ws[n][0], grads[n][0], ms[n][0], vs[n][0], "adamw_" + n.split("_", 1)[1])
        delta[n], new_m[n], new_v[n] = d_[None], m_[None], v_[None]
    small = [n for n in names if n not in big]
    width = 4 * LANES
    pw, poffs = _pack_rows([ws[n] for n in small], width)
    pg, _ = _pack_rows([grads[n] for n in small], width)
    pm, _ = _pack_rows([ms[n] for n in small], width)
    pv, _ = _pack_rows([vs[n] for n in small], width)
    sd, sm, sv = _adamw(pw, pg, pm, pv, "adamw_small")
    shapes = [ws[n].shape for n in small]
    for n, d_, m_, v_ in zip(small, _unpack_rows(sd, poffs, shapes), _unpack_rows(sm, poffs, shapes),
                             _unpack_rows(sv, poffs, shapes)):
        delta[n], new_m[n], new_v[n] = d_, m_, v_

    return (loss, grad_x, *[grads[n] for n in names], *[delta[n] for n in names], *[new_m[n] for n in names],
            *[new_v[n] for n in names])
```

```python
import jax
import jax.numpy as jnp
from jax import lax
from jax.experimental import pallas as pl
from jax.experimental.pallas import tpu as pltpu

F32 = jnp.float32
BF16 = jnp.bfloat16
MESH = pl.DeviceIdType.MESH
ALL_AXES = ("x", "y", "c")

N_META = 16
POOL_WINDOWS = (2, 4, 8, 16)
LN_EPS = 1e-5
ALPHA = 2.0 ** 0.25
ADAM_LR, ADAM_B1, ADAM_B2, ADAM_EPS, ADAM_WD, ADAM_STEP = 0.001, 0.9, 0.999, 1e-08, 0.01, 10

VMEM_LIMIT_V7X = 60 * 1024 * 1024
LANES = 128
ELEMENTWISE_BLOCK_BYTES = 2 * 1024 * 1024


def _params(**kw):
    return pltpu.CompilerParams(vmem_limit_bytes=VMEM_LIMIT_V7X, **kw)


def _tile(n, prefs):
    for t in prefs:
        if t <= n and n % t == 0:
            return t
    return n


def _rows_tile(rows, cols, mult=16):
    cap = max(mult, ELEMENTWISE_BLOCK_BYTES // (4 * cols))
    best = None
    for t in range(mult, min(rows, cap) + 1, mult):
        if rows % t == 0:
            best = t
    return best if best is not None else rows


def _matmul(a, b, *, mode, tm, tn, tk, out_dtype, name, a_outer=True, scale=1.0, after=None, k_off=0, nk=None,
            addend=None):
    if mode == "tn":
        K, M = a.shape
    else:
        M, K = a.shape
    N = b.shape[0] if mode == "nt" else b.shape[1]
    assert M % tm == 0 and N % tn == 0 and K % tk == 0, (name, M, N, K, tm, tn, tk)
    nk = K // tk if nk is None else nk
    if a_outer:
        grid = (M // tm, N // tn, nk)
        ij = lambda p, q: (p, q)
    else:
        grid = (N // tn, M // tm, nk)
        ij = lambda p, q: (q, p)

    if mode == "tn":
        a_spec = pl.BlockSpec((tk, tm), lambda p, q, k: (k + k_off, ij(p, q)[0]))
        dims = (((0,), (0,)), ((), ()))
    else:
        a_spec = pl.BlockSpec((tm, tk), lambda p, q, k: (ij(p, q)[0], k + k_off))
        dims = (((1,), (1,)), ((), ())) if mode == "nt" else (((1,), (0,)), ((), ()))
    if mode == "nt":
        b_spec = pl.BlockSpec((tn, tk), lambda p, q, k: (ij(p, q)[1], k + k_off))
    else:
        b_spec = pl.BlockSpec((tk, tn), lambda p, q, k: (k + k_off, ij(p, q)[1]))
    o_spec = pl.BlockSpec((tm, tn), lambda p, q, k: ij(p, q))

    extra, extra_specs = [], []
    if addend is not None:
        extra.append(addend)
        extra_specs.append(o_spec)
    if after is not None:
        extra.append(after)
        extra_specs.append(pl.BlockSpec(memory_space=pl.ANY))
    n_in = 2 + len(extra)

    def fin(acc, refs):
        if scale != 1.0:
            acc = acc * scale
        if addend is not None:
            acc = acc + refs[2][...]
        return acc.astype(out_dtype)

    if nk == 1:
        def body(*refs):
            a_ref, b_ref, o_ref = refs[0], refs[1], refs[n_in]
            o_ref[...] = fin(lax.dot_general(a_ref[...], b_ref[...], dims, preferred_element_type=F32), refs)
        scratch = []
    else:
        def body(*refs):
            a_ref, b_ref, o_ref, acc_ref = refs[0], refs[1], refs[n_in], refs[n_in + 1]
            k = pl.program_id(2)
            prod = lax.dot_general(a_ref[...], b_ref[...], dims, preferred_element_type=F32)

            @pl.when(k == 0)
            def _():
                acc_ref[...] = prod

            @pl.when(k > 0)
            def _():
                acc_ref[...] += prod

            @pl.when(k == nk - 1)
            def _():
                o_ref[...] = fin(acc_ref[...], refs)
        scratch = [pltpu.VMEM((tm, tn), F32)]

    return pl.pallas_call(
        body, name=name, grid=grid, in_specs=[a_spec, b_spec] + extra_specs, out_specs=o_spec,
        out_shape=jax.ShapeDtypeStruct((M, N), out_dtype), scratch_shapes=scratch,
        compiler_params=_params(),
    )(a, b, *extra)


TM_PREFS = (1376, 688, 512, 256, 128, 64)


def _mm_nn(a, b, out_dtype, name, after=None):
    M, K = a.shape
    N = b.shape[1]
    big_k = K * 2 * 1376 > 12 * 1024 * 1024
    tm = _tile(M, (688,) + TM_PREFS[2:]) if big_k else _tile(M, TM_PREFS)
    tn = _tile(N, (256, 128)) if big_k else _tile(N, (512, 256, 128))
    return _matmul(a, b, mode="nn", tm=tm, tn=tn, tk=K, out_dtype=out_dtype, name=name, after=after)


def _mm_nt(a, b, out_dtype, name, scale=1.0, after=None, mid=None):
    M, K = a.shape
    N = b.shape[0]
    mib = 1024 * 1024
    parts = 1
    while K // parts * 2 * 688 > 16 * mib and K % (2 * parts * LANES) == 0:
        parts *= 2
    kp = K // parts
    tm = _tile(M, TM_PREFS) if kp * 2 * 1376 <= 12 * mib else _tile(M, TM_PREFS[1:])
    tn = _tile(N, (512, 256, 128))
    if 4 * kp * (tm + tn) > 44 * mib:
        tn = _tile(N, (256, 128))
    acc = None
    for p in range(parts):
        last = p == parts - 1
        acc = _matmul(a, b, mode="nt", tm=tm, tn=tn, tk=kp, k_off=p, nk=1, out_dtype=out_dtype if last else F32,
                      name=name, scale=scale, after=after, addend=acc)
        after = mid(acc) if mid is not None and p == 0 and not last else None
    if mid is not None and parts == 1:
        mid(acc)
    return acc


def _mm_tn(a, b, name, after=None):
    T, M = a.shape
    N = b.shape[1]
    if M % 2048 == 0:
        tm, tn, a_outer = 2048, _tile(N, (512, 256, 128)), True
    else:
        tm, tn, a_outer = _tile(M, (256, 128)), _tile(N, (2048, 1024, 512, 256, 128)), False
    return _matmul(a, b, mode="tn", tm=tm, tn=tn, tk=T, out_dtype=BF16, name=name, a_outer=a_outer, after=after)


def _call(body, args, after, *, in_specs, **kw):
    if after is None:
        return pl.pallas_call(body, in_specs=in_specs, **kw)(*args)
    n = len(args)

    def ordered(*refs):
        return body(*refs[:n], *refs[n + 1:])

    return pl.pallas_call(ordered, in_specs=list(in_specs) + [pl.BlockSpec(memory_space=pl.ANY)], **kw)(*args, after)


def _silu_mul(gu, name, after=None):
    T, F2 = gu.shape
    F = F2 // 2
    tm = _rows_tile(T, F)

    def body(g_ref, u_ref, o_ref):
        g = g_ref[...].astype(F32)
        o_ref[...] = (g * jax.nn.sigmoid(g) * u_ref[...].astype(F32)).astype(BF16)

    return _call(
        body, (gu, gu), after, name=name, grid=(T // tm,),
        in_specs=[pl.BlockSpec((tm, F), lambda i: (i, 0)), pl.BlockSpec((tm, F), lambda i: (i, 1))],
        out_specs=pl.BlockSpec((tm, F), lambda i: (i, 0)),
        out_shape=jax.ShapeDtypeStruct((T, F), BF16), compiler_params=_params(),
    )


def _swiglu_bwd(da, gu, name, after=None):
    T, F2 = gu.shape
    F = F2 // 2
    tm = _rows_tile(T, F)

    def body(da_ref, g_ref, u_ref, o_ref):
        g = g_ref[...].astype(F32)
        da_ = da_ref[...].astype(F32)
        s = jax.nn.sigmoid(g)
        o_ref[:, :F] = (da_ * u_ref[...].astype(F32) * (s * (1.0 + g * (1.0 - s)))).astype(BF16)
        o_ref[:, F:] = (da_ * (g * s)).astype(BF16)

    return _call(
        body, (da, gu, gu), after, name=name, grid=(T // tm,),
        in_specs=[pl.BlockSpec((tm, F), lambda i: (i, 0)), pl.BlockSpec((tm, F), lambda i: (i, 0)),
                  pl.BlockSpec((tm, F), lambda i: (i, 1))],
        out_specs=pl.BlockSpec((tm, F2), lambda i: (i, 0)),
        out_shape=jax.ShapeDtypeStruct((T, F2), BF16), compiler_params=_params(),
    )


def _ln_stats(r):
    mu = jnp.mean(r, axis=-1, keepdims=True)
    xc = r - mu
    var = jnp.mean(xc * xc, axis=-1, keepdims=True)
    rstd = lax.rsqrt(var + LN_EPS)
    return xc * rstd, rstd


def _ln_bwd_math(dh, xhat, rstd, g):
    dxh = dh * g
    m1 = jnp.mean(dxh, axis=-1, keepdims=True)
    m2 = jnp.mean(dxh * xhat, axis=-1, keepdims=True)
    return rstd * (dxh - m1 - xhat * m2)


def _ln_fwd(hprev, f, g, b, fscale, name, after=None):
    T, D = hprev.shape
    tm = _rows_tile(T, D)

    def body(hp_ref, f_ref, g_ref, b_ref, h_ref, hb_ref, xh_ref, rs_ref):
        r = ALPHA * hp_ref[...] + fscale * f_ref[...]
        xhat, rstd = _ln_stats(r)
        h = xhat * g_ref[...] + b_ref[...]
        h_ref[...] = h
        hb_ref[...] = h.astype(BF16)
        xh_ref[...] = xhat
        rs_ref[...] = rstd

    row = pl.BlockSpec((tm, D), lambda i: (i, 0))
    vec = pl.BlockSpec((1, D), lambda i: (0, 0))
    col = pl.BlockSpec((tm, 1), lambda i: (i, 0))
    return _call(
        body, (hprev, f, g, b), after, name=name, grid=(T // tm,), in_specs=[row, row, vec, vec],
        out_specs=[row, row, row, col],
        out_shape=[jax.ShapeDtypeStruct((T, D), F32), jax.ShapeDtypeStruct((T, D), BF16),
                   jax.ShapeDtypeStruct((T, D), F32), jax.ShapeDtypeStruct((T, 1), F32)],
        compiler_params=_params(),
    )


def _ln_bwd(dr_next, dh_branch, xhat, rstd, g, fscale, name, after=None):
    T, D = xhat.shape
    tm = _rows_tile(T, D)

    def body(dn_ref, db_ref, xh_ref, rs_ref, g_ref, dr_ref, drb_ref, gs_ref, bs_ref):
        i = pl.program_id(0)
        dh = ALPHA * dn_ref[...] + db_ref[...]
        xhat_ = xh_ref[...]
        dr = _ln_bwd_math(dh, xhat_, rs_ref[...], g_ref[...])
        dr_ref[...] = dr
        drb_ref[...] = (fscale * dr).astype(BF16)

        @pl.when(i == 0)
        def _():
            gs_ref[...] = jnp.zeros_like(gs_ref)
            bs_ref[...] = jnp.zeros_like(bs_ref)

        gs_ref[...] += jnp.sum(dh * xhat_, axis=0, keepdims=True)
        bs_ref[...] += jnp.sum(dh, axis=0, keepdims=True)

    row = pl.BlockSpec((tm, D), lambda i: (i, 0))
    vec = pl.BlockSpec((1, D), lambda i: (0, 0))
    col = pl.BlockSpec((tm, 1), lambda i: (i, 0))
    return _call(
        body, (dr_next, dh_branch, xhat, rstd, g), after, name=name, grid=(T // tm,),
        in_specs=[row, row, row, col, vec], out_specs=[row, row, vec, vec],
        out_shape=[jax.ShapeDtypeStruct((T, D), F32), jax.ShapeDtypeStruct((T, D), BF16),
                   jax.ShapeDtypeStruct((1, D), F32), jax.ShapeDtypeStruct((1, D), F32)],
        compiler_params=_params(),
    )


def _ln3_loss(hprev, f, target, mask, g, b, name):
    T, D = hprev.shape
    tm = _rows_tile(T, D)

    def body(hp_ref, f_ref, t_ref, m_ref, g_ref, b_ref, dr_ref, drb_ref, gs_ref, bs_ref, ls_ref):
        i = pl.program_id(0)
        r = ALPHA * hp_ref[...] + 0.5 * f_ref[...]
        xhat, rstd = _ln_stats(r)
        g_ = g_ref[...]
        y = xhat * g_ + b_ref[...]
        err = (y - t_ref[...]) * m_ref[...]
        dy = err * (1.0 / D)
        dr = _ln_bwd_math(dy, xhat, rstd, g_)
        dr_ref[...] = dr
        drb_ref[...] = (0.5 * dr).astype(BF16)

        @pl.when(i == 0)
        def _():
            gs_ref[...] = jnp.zeros_like(gs_ref)
            bs_ref[...] = jnp.zeros_like(bs_ref)
            ls_ref[...] = jnp.zeros_like(ls_ref)

        gs_ref[...] += jnp.sum(dy * xhat, axis=0, keepdims=True)
        bs_ref[...] += jnp.sum(dy, axis=0, keepdims=True)
        ls_ref[...] += jnp.sum(err * err, axis=0, keepdims=True)

    row = pl.BlockSpec((tm, D), lambda i: (i, 0))
    vec = pl.BlockSpec((1, D), lambda i: (0, 0))
    col = pl.BlockSpec((tm, 1), lambda i: (i, 0))
    return pl.pallas_call(
        body, name=name, grid=(T // tm,), in_specs=[row, row, row, col, vec, vec],
        out_specs=[row, row, vec, vec, vec],
        out_shape=[jax.ShapeDtypeStruct((T, D), F32), jax.ShapeDtypeStruct((T, D), BF16),
                   jax.ShapeDtypeStruct((1, D), F32), jax.ShapeDtypeStruct((1, D), F32),
                   jax.ShapeDtypeStruct((1, D), F32)],
        compiler_params=_params(),
    )(hprev, f, target, mask, g, b)


def _down(x, k):
    rows = lax.broadcasted_iota(jnp.int32, x.shape, 0)
    return jnp.where(rows >= k, pltpu.roll(x, k, axis=0), 0.0)


def _up(x, k):
    n = x.shape[0]
    rows = lax.broadcasted_iota(jnp.int32, x.shape, 0)
    return jnp.where(rows < n - k, pltpu.roll(x, n - k, axis=0), 0.0)


def _conv_fwd(u, conv_w, L, CC, name, after=None):
    T = u.shape[0]
    tc = _tile(CC, (256, 128))
    n = CC // tc

    def body(b_ref, c_ref, x_ref, w_ref, o_ref):
        cx = c_ref[...] * x_ref[...]
        w = w_ref[...]
        conv = w[0:1] * _down(cx, 2) + w[1:2] * _down(cx, 1) + w[2:3] * cx
        o_ref[...] = (b_ref[...] * conv).astype(BF16)

    blk = lambda off: pl.BlockSpec((L, tc), lambda s, j: (s, j + off * n))
    return _call(
        body, (u, u, u, conv_w), after, name=name, grid=(T // L, n),
        in_specs=[blk(0), blk(1), blk(2), pl.BlockSpec((3, tc), lambda s, j: (0, j))],
        out_specs=pl.BlockSpec((L, tc), lambda s, j: (s, j)),
        out_shape=jax.ShapeDtypeStruct((T, CC), BF16), compiler_params=_params(),
    )


def _conv_bwd(u, dy, conv_w, L, CC, name, after=None):
    T = u.shape[0]
    tc = _tile(CC, (256, 128))
    n = CC // tc

    def body(b_ref, c_ref, x_ref, dy_ref, w_ref, db_ref, dc_ref, dx_ref, dw_ref):
        s = pl.program_id(1)
        c_, x_ = c_ref[...], x_ref[...]
        cx = c_ * x_
        w = w_ref[...]
        cx1, cx2 = _down(cx, 1), _down(cx, 2)
        conv = w[0:1] * cx2 + w[1:2] * cx1 + w[2:3] * cx
        dy_ = dy_ref[...]
        db_ref[...] = (dy_ * conv).astype(BF16)
        dconv = dy_ * b_ref[...]
        dcx = w[2:3] * dconv + w[1:2] * _up(dconv, 1) + w[0:1] * _up(dconv, 2)
        dc_ref[...] = (dcx * x_).astype(BF16)
        dx_ref[...] = (dcx * c_).astype(BF16)

        @pl.when(s == 0)
        def _():
            dw_ref[...] = jnp.zeros_like(dw_ref)

        dw_ref[0:1, :] += jnp.sum(dconv * cx2, axis=0, keepdims=True)
        dw_ref[1:2, :] += jnp.sum(dconv * cx1, axis=0, keepdims=True)
        dw_ref[2:3, :] += jnp.sum(dconv * cx, axis=0, keepdims=True)

    blk = lambda off: pl.BlockSpec((L, tc), lambda j, s: (s, j + off * n))
    out = pl.BlockSpec((L, tc), lambda j, s: (s, j))
    act = jax.ShapeDtypeStruct((T, CC), BF16)
    return _call(
        body, (u, u, u, dy, conv_w), after, name=name, grid=(n, T // L),
        in_specs=[blk(0), blk(1), blk(2), blk(0), pl.BlockSpec((3, tc), lambda j, s: (0, j))],
        out_specs=[out, out, out, pl.BlockSpec((3, tc), lambda j, s: (0, j))],
        out_shape=[act, act, act, jax.ShapeDtypeStruct((3, CC), F32)], compiler_params=_params(),
    )


def _pool_counts(shape, g):
    rows = lax.broadcasted_iota(jnp.int32, shape, 0)
    win = jnp.left_shift(jnp.int32(POOL_WINDOWS[0]), g)
    return jnp.minimum(rows + 1, win).astype(F32)


def _pick_group(g, vals):
    out = vals[-1]
    for i in range(len(vals) - 2, -1, -1):
        out = jnp.where(g == i, vals[i], out)
    return out


def _pool_fwd(u, pool_w, pool_scale, L, CC, PG, name):
    T = u.shape[0]
    G = len(POOL_WINDOWS)
    off = 3 * CC // PG

    def body(z_ref, w_ref, sc_ref, y_ref, d_ref):
        g = pl.program_id(1)
        z = z_ref[...]
        s1 = z + _down(z, 1)
        s2 = s1 + _down(s1, 2)
        s3 = s2 + _down(s2, 4)
        s4 = s3 + _down(s3, 8)
        pooled = _pick_group(g, [s1, s2, s3, s4]) / _pool_counts(z.shape, g)
        d = (pooled - z).astype(BF16)
        d_ref[...] = d
        q = jnp.dot(d, w_ref[...], preferred_element_type=F32)
        y_ref[...] = (q * sc_ref[...]).astype(BF16)

    blk = pl.BlockSpec((L, PG), lambda s, g: (s, g))
    act = jax.ShapeDtypeStruct((T, G * PG), BF16)
    return pl.pallas_call(
        body, name=name, grid=(T // L, G),
        in_specs=[pl.BlockSpec((L, PG), lambda s, g: (s, off + g)), pl.BlockSpec((None, PG, PG), lambda s, g: (g, 0, 0)),
                  pl.BlockSpec((1, PG), lambda s, g: (0, g))],
        out_specs=[blk, blk], out_shape=[act, act], compiler_params=_params(),
    )(u, pool_w, pool_scale)


def _pool_bwd(d, dy, pool_w, pool_scale, L, CC, PG, name):
    T = d.shape[0]
    G = len(POOL_WINDOWS)
    off = CC // PG
    PGs = PG // 4

    def body(d_ref, dy_ref, w_ref, sc_ref, dz_ref, dw_ref, dsc_ref):
        g = pl.program_id(0)
        s = pl.program_id(1)
        d_ = d_ref[...]
        w = w_ref[...]
        dy_ = dy_ref[...]
        q = jnp.dot(d_, w, preferred_element_type=F32)
        dq = (dy_ * sc_ref[...]).astype(BF16)
        dd = lax.dot_general(dq, w, (((1,), (1,)), ((), ())), preferred_element_type=F32)
        e = dd / _pool_counts(dd.shape, g)
        a1 = e + _up(e, 1)
        a2 = a1 + _up(a1, 2)
        a3 = a2 + _up(a2, 4)
        a4 = a3 + _up(a3, 8)
        dz_ref[...] = (_pick_group(g, [a1, a2, a3, a4]) - dd).astype(BF16)

        @pl.when(s == 0)
        def _():
            dw_ref[...] = jnp.zeros_like(dw_ref)
            dsc_ref[...] = jnp.zeros_like(dsc_ref)

        dw = lax.dot_general(d_, dq, (((0,), (0,)), ((), ())), preferred_element_type=F32)
        for k in range(4):
            dw_ref[k] += dw[k * PGs:(k + 1) * PGs, :]
        dsc_ref[...] += jnp.sum(dy_ * q, axis=0, keepdims=True)

    return pl.pallas_call(
        body, name=name, grid=(G, T // L),
        in_specs=[pl.BlockSpec((L, PG), lambda g, s: (s, g)), pl.BlockSpec((L, PG), lambda g, s: (s, off + g)),
                  pl.BlockSpec((None, PG, PG), lambda g, s: (g, 0, 0)), pl.BlockSpec((1, PG), lambda g, s: (0, g))],
        out_specs=[pl.BlockSpec((L, PG), lambda g, s: (s, g)),
                   pl.BlockSpec((4, None, PGs, PG), lambda g, s: (0, g, 0, 0)),
                   pl.BlockSpec((1, PG), lambda g, s: (0, g))],
        out_shape=[jax.ShapeDtypeStruct((T, G * PG), BF16), jax.ShapeDtypeStruct((4, G, PGs, PG), F32),
                   jax.ShapeDtypeStruct((1, G * PG), F32)],
        compiler_params=_params(),
    )(d, dy, pool_w, pool_scale)


def _input_grad(dr1, dh_branch, L, name, after=None):
    T, D = dr1.shape
    nseq = T // L
    tc = _tile(D, (512, 256, 128))

    def body(a_ref, b_ref, gx_ref, gm_ref):
        dh = ALPHA * a_ref[...] + b_ref[...]
        gm_ref[...] = dh[:N_META]
        gx_ref[...] = dh[N_META:]

    blk = pl.BlockSpec((L, tc), lambda s, j: (s, j))
    return _call(
        body, (dr1, dh_branch), after, name=name, grid=(nseq, D // tc), in_specs=[blk, blk],
        out_specs=[pl.BlockSpec((None, L - N_META, tc), lambda s, j: (s, 0, j)),
                   pl.BlockSpec((None, N_META, tc), lambda s, j: (s, 0, j))],
        out_shape=[jax.ShapeDtypeStruct((nseq, L - N_META, D), F32), jax.ShapeDtypeStruct((nseq, N_META, D), F32)],
        compiler_params=_params(),
    )


def _chip():
    return 2 * lax.axis_index("x") + lax.axis_index("y")


def _core():
    return lax.axis_index("c")


def _cast_into_full(w, sh, name):
    tr = _rows_tile(sh.srows, sh.scols)
    nb = sh.srows // tr

    def body(w_ref, o_ref):
        o_ref[...] = w_ref[...].astype(BF16)

    if sh.axis == 0:
        out = pl.BlockSpec((tr, sh.scols), lambda i: (_chip() * nb + i, 0))
    else:
        out = pl.BlockSpec((tr, sh.scols), lambda i: (i, _chip()))
    return pl.pallas_call(
        body, name=name, grid=(nb,), in_specs=[pl.BlockSpec((tr, sh.scols), lambda i: (i, 0))], out_specs=out,
        out_shape=jax.ShapeDtypeStruct((sh.rows, sh.cols), BF16), compiler_params=_params(),
    )(w)


def _add_halves(dw, theirs, sh, name):
    R, C = sh.half_all_shape
    tr = _rows_tile(sh.h, C)
    nh, ns = sh.h // tr, sh.srows // tr

    def body(a_ref, b_ref, o_ref):
        o_ref[...] = (a_ref[...].astype(F32) + b_ref[...].astype(F32)).astype(BF16)

    blk = pl.BlockSpec((tr, C), lambda i: (i, 0))
    mine = pl.BlockSpec((tr, C), lambda i: ((i // nh) * ns + _core() * nh + i % nh, 0))
    return pl.pallas_call(body, name=name, grid=(R // tr,), in_specs=[mine, blk], out_specs=blk,
                          out_shape=jax.ShapeDtypeStruct((R, C), BF16), compiler_params=_params())(dw, theirs)


def _sum_parts(part, recv, sh, name):
    h, C = sh.half_shape
    tr = _rows_tile(h, C)
    nh = h // tr

    def body(o_ref, r_ref, out_ref):
        acc = o_ref[...].astype(F32)
        for k in range(3):
            acc = acc + r_ref[k].astype(F32)
        out_ref[...] = acc

    if sh.axis == 0:
        own = pl.BlockSpec((tr, C), lambda i: (_chip() * nh + i, 0))
    else:
        own = pl.BlockSpec((tr, C), lambda i: (i, _chip()))
    return pl.pallas_call(
        body, name=name, grid=(nh,), in_specs=[own, pl.BlockSpec((3, tr, C), lambda i: (0, i, 0))],
        out_specs=pl.BlockSpec((tr, C), lambda i: (_core() * nh + i, 0)),
        out_shape=jax.ShapeDtypeStruct((sh.srows, sh.scols), F32), compiler_params=_params(),
    )(part, recv)


def _adamw(w, g, m, v, name, copy_grad=False):
    R, C = w.shape
    tr = _rows_tile(R, C, mult=8)

    def body(w_ref, g_ref, m_ref, v_ref, d_ref, nm_ref, nv_ref, *g_out):
        g_ = g_ref[...]
        for o in g_out:
            o[...] = g_
        m_ = ADAM_B1 * m_ref[...] + (1.0 - ADAM_B1) * g_
        v_ = ADAM_B2 * v_ref[...] + (1.0 - ADAM_B2) * (g_ * g_)
        m_hat = m_ / (1.0 - ADAM_B1 ** ADAM_STEP)
        v_hat = v_ / (1.0 - ADAM_B2 ** ADAM_STEP)
        d_ref[...] = -ADAM_LR * (m_hat / (jnp.sqrt(v_hat) + ADAM_EPS) + ADAM_WD * w_ref[...])
        nm_ref[...] = m_
        nv_ref[...] = v_

    blk = pl.BlockSpec((tr, C), lambda i: (i, 0))
    shp = jax.ShapeDtypeStruct((R, C), F32)
    n_out = 4 if copy_grad else 3
    return pl.pallas_call(body, name=name, grid=(R // tr,), in_specs=[blk] * 4, out_specs=[blk] * n_out,
                          out_shape=[shp] * n_out, compiler_params=_params())(w, g, m, v)


def _pos():
    return lax.axis_index("x"), lax.axis_index("y"), lax.axis_index("c")


def _other_chips(x, y):
    return [(1 - x, y), (x, 1 - y), (1 - x, 1 - y)]


class _Sharded:
    def __init__(self, full_shape, axis):
        self.rows, self.cols = full_shape
        self.axis = axis
        self.srows = self.rows // 4 if axis == 0 else self.rows
        self.scols = self.cols if axis == 0 else self.cols // 4
        assert self.srows % 2 == 0
        self.h = self.srows // 2
        self.half_all_shape = (4 * self.h, self.cols) if axis == 0 else (self.h, self.cols)
        self.half_shape = (self.h, self.scols)

    def shard(self, ref, q):
        if self.axis == 0:
            return ref.at[pl.ds(q * self.srows, self.srows), :]
        return ref.at[:, pl.ds(pl.multiple_of(q * self.scols, LANES), self.scols)]

    def half(self, ref, q, c):
        if self.axis == 0:
            return ref.at[pl.ds(q * self.srows + c * self.h, self.h), :]
        return ref.at[pl.ds(c * self.h, self.h), pl.ds(pl.multiple_of(q * self.scols, LANES), self.scols)]

    def half_in_stack(self, ref, q):
        if self.axis == 0:
            return ref.at[pl.ds(q * self.h, self.h), :]
        return ref.at[:, pl.ds(pl.multiple_of(q * self.scols, LANES), self.scols)]


ANY = pl.BlockSpec(memory_space=pl.ANY)
DMA_ROW_ALIGN = 16
CHUNK_COUNTS = (16, 8, 43, 4, 2)


def _row_chunks(rows):
    n = 1
    if rows % DMA_ROW_ALIGN == 0:
        n = _tile(rows // DMA_ROW_ALIGN, CHUNK_COUNTS)
        n = n if n in CHUNK_COUNTS else 1
    size = rows // n
    return [(k * size, size) for k in range(n)]


class _Pieces:
    def __init__(self, src, dst, send_sem, recv_sem, to):
        self.args = (src, dst, send_sem, recv_sem, to)

    def _copy(self, rows=None):
        src, dst, send_sem, recv_sem, to = self.args
        if rows is not None:
            src, dst = src.at[pl.ds(*rows), :], dst.at[pl.ds(*rows), :]
        return pltpu.make_async_remote_copy(src_ref=src, dst_ref=dst, send_sem=send_sem, recv_sem=recv_sem,
                                            device_id=to, device_id_type=MESH)

    def start(self):
        for rows in _row_chunks(self.args[0].shape[0]):
            self._copy(rows).start()

    def wait_send(self):
        self._copy().wait_send()

    def wait_recv(self):
        self._copy().wait_recv()

    def wait(self):
        self._copy().wait()


HBM = pl.BlockSpec(memory_space=pltpu.HBM)
SEM = pl.BlockSpec(memory_space=pltpu.SEMAPHORE)
SPLIT_COPY = pltpu.CompilerParams(has_side_effects=pltpu.SideEffectType.DATAFLOW_SIDE_EFFECTING)
TOKEN = jax.ShapeDtypeStruct((8, LANES), F32)


def _in_hbm(a):
    return pltpu.with_memory_space_constraint(a, pltpu.HBM)


def _split_start(arrays, plan, n, name):
    na = len(arrays)

    def body(*refs):
        send_sems, recv_sems, token = refs[na], refs[na + 1], refs[-1]
        x, y, c = _pos()
        for j, (src, dst, to, _) in enumerate(plan(refs[:na], x, y, c)):
            _Pieces(src, dst, send_sems.at[j], recv_sems.at[j], to).start()
        token[...] = jnp.zeros_like(token)

    outs = pl.pallas_call(
        body, name=name, in_specs=(HBM,) * na,
        out_shape=(pltpu.SemaphoreType.DMA((n,)), pltpu.SemaphoreType.DMA((n,)),
                   *[pltpu.HBM(a.shape, a.dtype) for a in arrays], TOKEN),
        out_specs=(SEM, SEM) + (HBM,) * na + (pl.BlockSpec(memory_space=pltpu.VMEM),),
        input_output_aliases={i: 2 + i for i in range(na)}, compiler_params=SPLIT_COPY,
    )(*[_in_hbm(a) for a in arrays])
    return (outs[0], outs[1], list(outs[2:2 + na])), outs[-1]


def _split_wait(state, after, plan, name):
    send_sems, recv_sems, arrays = state
    na = len(arrays)

    def body(*refs):
        send_sems, recv_sems = refs[na], refs[na + 1]
        x, y, c = _pos()
        for j, (src, _, to, landing) in enumerate(plan(refs[:na], x, y, c)):
            cp = pltpu.make_async_remote_copy(src_ref=src, dst_ref=landing, send_sem=send_sems.at[j],
                                              recv_sem=recv_sems.at[j], device_id=to, device_id_type=MESH)
            cp.wait_send()
            cp.wait_recv()

    outs = pl.pallas_call(
        body, name=name, in_specs=(HBM,) * na + (SEM, SEM, ANY),
        out_shape=tuple(pltpu.HBM(a.shape, a.dtype) for a in arrays), out_specs=(HBM,) * na,
        input_output_aliases={i: i for i in range(na)}, compiler_params=SPLIT_COPY,
    )(*arrays, send_sems, recv_sems, after)
    return list(outs)


def _plan_gather_ici(sh):
    def plan(refs, x, y, c):
        (full,) = refs
        mine = sh.half(full, 2 * x + y, c)
        return [(mine, mine, (cx, cy, c), sh.half(full, 2 * cx + cy, c)) for cx, cy in _other_chips(x, y)]
    return plan


def _plan_gather_d2d(sh):
    def plan(refs, x, y, c):
        (full,) = refs
        out = []
        for cx, cy in _other_chips(x, y):
            landed = sh.half(full, 2 * cx + cy, c)
            out.append((landed, landed, (x, y, 1 - c), sh.half(full, 2 * cx + cy, 1 - c)))
        return out
    return plan


def _plan_rs_sibling(sh):
    def plan(refs, x, y, c):
        dw, theirs = refs
        if sh.axis == 1:
            return [(dw.at[pl.ds((1 - c) * sh.h, sh.h), :], theirs, (x, y, 1 - c), theirs)]
        out = []
        for k in range(4):
            give = dw.at[pl.ds(k * sh.srows + (1 - c) * sh.h, sh.h), :]
            theirs_k = theirs.at[pl.ds(k * sh.h, sh.h), :]
            out.append((give, theirs_k, (x, y, 1 - c), theirs_k))
        return out
    return plan


def _plan_rs_ici(sh):
    def plan(refs, x, y, c):
        part, land = refs
        return [(sh.half_in_stack(part, 2 * cx + cy), land.at[j], (cx, cy, c), land.at[j])
                for j, (cx, cy) in enumerate(_other_chips(x, y))]
    return plan


def _plan_share(sh):
    def plan(refs, x, y, c):
        (g,) = refs
        mine = g.at[pl.ds(c * sh.h, sh.h), :]
        return [(mine, mine, (x, y, 1 - c), g.at[pl.ds((1 - c) * sh.h, sh.h), :])]
    return plan


def _exchange_small(v, reduce, name, after=None):
    R, C = v.shape

    def body(v_ref, o_ref, buf, send_sems, recv_sems):
        x, y, c = _pos()
        me = 4 * x + 2 * y + c
        slots = buf if reduce else o_ref
        slots[me] = v_ref[...]
        cps = []
        for k in range(1, 8):
            px = 1 - x if k & 4 else x
            py = 1 - y if k & 2 else y
            pc = 1 - c if k & 1 else c
            cps.append((_Pieces(v_ref, slots.at[me], send_sems.at[k - 1], recv_sems.at[k - 1], (px, py, pc)),
                        4 * px + 2 * py + pc))
        for cp, _ in cps:
            cp.start()
        for k, (cp, peer) in enumerate(cps):
            pltpu.make_async_remote_copy(
                src_ref=v_ref, dst_ref=slots.at[peer], send_sem=send_sems.at[k], recv_sem=recv_sems.at[k],
                device_id=(x, y, c), device_id_type=MESH).wait_recv()
        if reduce:
            acc = buf[0]
            for d in range(1, 8):
                acc = acc + buf[d]
            o_ref[...] = acc
        for cp, _ in cps:
            cp.wait_send()

    vm = pl.BlockSpec(memory_space=pltpu.VMEM)
    out_shape = jax.ShapeDtypeStruct((R, C) if reduce else (8, R, C), v.dtype)
    scratch = [pltpu.VMEM((8, R, C) if reduce else (8, LANES), v.dtype),
               pltpu.SemaphoreType.DMA((7,)), pltpu.SemaphoreType.DMA((7,))]
    return _call(body, (v,), after, name=name, in_specs=[vm], out_specs=vm, out_shape=out_shape,
                 scratch_shapes=scratch, compiler_params=_params())


def _pack_rows(parts, width):
    rows, offs, at = [], [], 0
    for p in parts:
        flat = p.reshape(-1)
        n = 8 * -(-flat.shape[0] // (8 * width))
        flat = jnp.pad(flat, (0, n * width - flat.shape[0]))
        rows.append(flat.reshape(n, width))
        offs.append(at)
        at += n
    return jnp.concatenate(rows, axis=0), offs


def _unpack_rows(packed, offs, shapes):
    out = []
    for off, shp in zip(offs, shapes):
        size = 1
        for s in shp:
            size *= s
        n = -(-size // packed.shape[1])
        out.append(packed[off:off + n].reshape(-1)[:size].reshape(shp))
    return out


def kernel(x, meta_tokens, ffn1_w_gu, ffn1_w_down, ln1_g, ln1_b, w_in, conv_w, pool_w, pool_scale, w_out, ln2_g, ln2_b, ffn2_w_gu, ffn2_w_down, ln3_g, ln3_b, loss_target, m_meta_tokens, m_ffn1_w_gu, m_ffn1_w_down, m_ln1_g, m_ln1_b, m_w_in, m_conv_w, m_pool_w, m_pool_scale, m_w_out, m_ln2_g, m_ln2_b, m_ffn2_w_gu, m_ffn2_w_down, m_ln3_g, m_ln3_b, v_meta_tokens, v_ffn1_w_gu, v_ffn1_w_down, v_ln1_g, v_ln1_b, v_w_in, v_conv_w, v_pool_w, v_pool_scale, v_w_out, v_ln2_g, v_ln2_b, v_ffn2_w_gu, v_ffn2_w_down, v_ln3_g, v_ln3_b):
    nseq, S, D = x.shape
    L = S + N_META
    T = nseq * L
    F = ffn1_w_down.shape[1] * 4
    CC = conv_w.shape[2] * 4
    G, PGs, PG = pool_w.shape[1:]
    PC = G * PG
    IN = w_in.shape[2] * 4
    qx, qy, qc = _pos()
    q = 2 * qx + qy

    sh_gu = _Sharded((D, 2 * F), 1)
    sh_down = _Sharded((F, D), 0)
    sh_in = _Sharded((D, IN), 1)
    sh_out = _Sharded((CC + PC, D), 0)
    sh_pool = _Sharded((4 * G * PGs, PG), 0)

    gathers = {}
    tok = jnp.zeros((), F32)
    for tag, w, sh in (("gu1", ffn1_w_gu, sh_gu), ("d1", ffn1_w_down, sh_down), ("in", w_in, sh_in),
                       ("out", w_out, sh_out), ("gu2", ffn2_w_gu, sh_gu), ("d2", ffn2_w_down, sh_down)):
        full = _cast_into_full(w[0], sh, "cast_" + tag)
        state, t_ = _split_start([full], _plan_gather_ici(sh), 3, "ag_start_" + tag)
        gathers[tag] = (state, sh)
        tok = tok + t_[0, 0]

    def landed(tag, after):
        state, sh = gathers.pop(tag)
        (full,) = _split_wait(state, after, _plan_gather_ici(sh), "ag_wait_" + tag)
        state, t_ = _split_start([full], _plan_gather_d2d(sh), 3, "ag_pass_" + tag)
        gathers[tag] = (state, sh)
        return t_

    def arrive(tag, after):
        state, sh = gathers.pop(tag)
        return _split_wait(state, after, _plan_gather_d2d(sh), "ag_got_" + tag)[0]

    small_w = [meta_tokens, conv_w[0], pool_w[0]]
    packed, offs = _pack_rows(small_w, 4 * LANES)
    slots = _exchange_small(packed, False, "ag_small")
    per_chip = [_unpack_rows(slots[2 * k], offs, [p.shape for p in small_w]) for k in range(4)]
    meta_full = jnp.concatenate([p[0] for p in per_chip], axis=1)
    conv_full = jnp.concatenate([p[1] for p in per_chip], axis=1)
    poolw_full = jnp.concatenate([p[2] for p in per_chip], axis=1).astype(BF16)
    pscale = pool_scale

    h0 = jnp.concatenate([jnp.broadcast_to(meta_full[None], (nseq, N_META, D)), x], axis=1).reshape(T, D)
    h0b = (h0 + tok).astype(BF16)
    tgt = jnp.pad(loss_target, ((0, 0), (N_META, 0), (0, 0))).reshape(T, D)
    mask = (lax.broadcasted_iota(jnp.int32, (nseq, L, 1), 1) >= N_META).astype(F32).reshape(T, 1)

    wgu1 = arrive("gu1", landed("gu1", h0b))
    gu1 = _mm_nn(h0b, wgu1, BF16, "ffn_gu")
    a1 = _silu_mul(gu1, "silu_mul", after=landed("d1", gu1))
    wd1 = arrive("d1", a1)
    f1 = _mm_nn(a1, wd1, F32, "ffn_down")
    h1, h1b, xh1, rs1 = _ln_fwd(h0, f1, ln1_g, ln1_b, 0.5, "ln_fwd", after=landed("in", f1))
    win = arrive("in", h1b)

    u = _mm_nn(h1b, win, F32, "mix_in")
    yc = _conv_fwd(u, conv_full, L, CC, "conv_fwd", after=landed("out", u))
    yp, dpool = _pool_fwd(u, poolw_full, pscale, L, CC, PG, "pool_fwd")
    ymix = jnp.concatenate([yc, yp], axis=1)
    wout = arrive("out", ymix)
    o = _mm_nn(ymix, wout, F32, "mix_out")
    h2, h2b, xh2, rs2 = _ln_fwd(h1, o, ln2_g, ln2_b, 1.0, "ln_fwd", after=landed("gu2", o))
    wgu2 = arrive("gu2", h2b)

    gu2 = _mm_nn(h2b, wgu2, BF16, "ffn_gu")
    a2 = _silu_mul(gu2, "silu_mul", after=landed("d2", gu2))
    wd2 = arrive("d2", a2)
    f2 = _mm_nn(a2, wd2, F32, "ffn_down")
    dr3, dr3b, gs3, bs3, lsum = _ln3_loss(h2, f2, tgt, mask, ln3_g, ln3_b, "ln3_loss")
    loss = lax.psum(0.5 * jnp.sum(lsum) / D, ALL_AXES)

    sibs, icis, shares = {}, {}, {}

    def sib_begin(dw, sh, tag):
        theirs = lax.empty(sh.half_all_shape, dw.dtype)
        state, t_ = _split_start([dw, theirs], _plan_rs_sibling(sh), 4 if sh.axis == 0 else 1, "rs_sib_" + tag)
        sibs[tag] = (state, sh)
        return t_

    def ici_begin(tag, after):
        state, sh = sibs.pop(tag)
        dw, theirs = _split_wait(state, after, _plan_rs_sibling(sh), "rs_sibw_" + tag)
        part = _add_halves(dw, theirs, sh, "rs_add_" + tag)
        land = lax.empty((3,) + sh.half_shape, part.dtype)
        state, t_ = _split_start([part, land], _plan_rs_ici(sh), 3, "rs_start_" + tag)
        icis[tag] = (state, sh)
        return t_

    def share_begin(tag, after):
        state, sh = icis.pop(tag)
        part, recv = _split_wait(state, after, _plan_rs_ici(sh), "rs_wait_" + tag)
        g = _sum_parts(part, recv, sh, "rs_sum_" + tag)
        state, t_ = _split_start([g], _plan_share(sh), 1, "rs_share_" + tag)
        shares[tag] = (state, sh)
        return t_

    def share_end(tag, after):
        state, sh = shares.pop(tag)
        return _split_wait(state, after, _plan_share(sh), "rs_got_" + tag)[0]

    def ffn_bwd(drb, a, gu, hb, wgu, wd, sfx):
        da = _mm_nt(drb, wd, BF16, "ffn_da")
        dwd = _mm_tn(a, drb, "ffn_dwd")
        dgu = _swiglu_bwd(da, gu, "swiglu_bwd", after=sib_begin(dwd, sh_down, "d" + sfx))
        dwgu = _mm_tn(hb, dgu, "ffn_dwgu", after=ici_begin("d" + sfx, dgu))
        return _mm_nt(dgu, wgu, F32, "ffn_dh", after=sib_begin(dwgu, sh_gu, "gu" + sfx),
                      mid=lambda part: ici_begin("gu" + sfx, part))

    dh2 = ffn_bwd(dr3b, a2, gu2, h2b, wgu2, wd2, "2")

    dr2, dr2b, gs2, bs2 = _ln_bwd(dr3, dh2, xh2, rs2, ln2_g, 1.0, "ln_bwd")
    dymix = _mm_nt(dr2b, wout, F32, "mix_dy")
    dwout = _mm_tn(ymix, dr2b, "mix_dwout")
    db_, dc_, dx_, dconvw = _conv_bwd(u, dymix, conv_full, L, CC, "conv_bwd", after=sib_begin(dwout, sh_out, "out"))
    dz_, dpoolw, dpscale = _pool_bwd(dpool, dymix, poolw_full, pscale, L, CC, PG, "pool_bwd")
    t_sp = sib_begin(dpoolw.reshape(4 * G * PGs, PG).astype(BF16), sh_pool, "pool")
    du = jnp.concatenate([db_, dc_, dx_, dz_], axis=1)
    dwin = _mm_tn(h1b, du, "mix_dwin", after=ici_begin("out", du) + t_sp)
    t_s = sib_begin(dwin, sh_in, "in")
    dh1 = _mm_nt(du, win, F32, "mix_dh", after=t_s + ici_begin("pool", dwin))

    dr1, dr1b, gs1, bs1 = _ln_bwd(dr2, dh1, xh1, rs1, ln1_g, 0.5, "ln_bwd", after=ici_begin("in", dh1))
    dh0f = ffn_bwd(dr1b, a1, gu1, h0b, wgu1, wd1, "1")
    grad_x, gmeta = _input_grad(dr1, dh0f, L, "input_grad")

    names = ["meta_tokens", "ffn1_w_gu", "ffn1_w_down", "ln1_g", "ln1_b", "w_in", "conv_w", "pool_w", "pool_scale",
             "w_out", "ln2_g", "ln2_b", "ffn2_w_gu", "ffn2_w_down", "ln3_g", "ln3_b"]
    ws = dict(zip(names, [meta_tokens, ffn1_w_gu, ffn1_w_down, ln1_g, ln1_b, w_in, conv_w, pool_w, pool_scale, w_out,
                          ln2_g, ln2_b, ffn2_w_gu, ffn2_w_down, ln3_g, ln3_b]))
    ms = dict(zip(names, [m_meta_tokens, m_ffn1_w_gu, m_ffn1_w_down, m_ln1_g, m_ln1_b, m_w_in, m_conv_w, m_pool_w,
                          m_pool_scale, m_w_out, m_ln2_g, m_ln2_b, m_ffn2_w_gu, m_ffn2_w_down, m_ln3_g, m_ln3_b]))
    vs = dict(zip(names, [v_meta_tokens, v_ffn1_w_gu, v_ffn1_w_down, v_ln1_g, v_ln1_b, v_w_in, v_conv_w, v_pool_w,
                          v_pool_scale, v_w_out, v_ln2_g, v_ln2_b, v_ffn2_w_gu, v_ffn2_w_down, v_ln3_g, v_ln3_b]))
    grads, delta, new_m, new_v = {}, {}, {}, {}

    weight_of = {"d2": "ffn2_w_down", "gu2": "ffn2_w_gu", "out": "w_out", "pool": "pool_w", "in": "w_in",
                 "d1": "ffn1_w_down", "gu1": "ffn1_w_gu"}
    big = tuple(n for n in weight_of.values() if n != "pool_w")

    def finish(tag, after):
        n = weight_of[tag]
        g = share_end(tag, after)
        if n == "pool_w":
            grads[n] = g.reshape(1, G, PGs, PG)
            return g
        d_, m_, v_, g_ = _adamw(ws[n][0], g, ms[n][0], vs[n][0], "adamw_" + tag, copy_grad=True)
        grads[n], delta[n], new_m[n], new_v[n] = g_[None], d_[None], m_[None], v_[None]
        return d_

    last, prev = None, grad_x
    for tag in weight_of:
        t_ = share_begin(tag, prev)
        prev = t_ if last is None else finish(last, t_)
        last = tag

    small_g = [gs1, bs1, gs2, bs2, gs3, bs3, dpscale, dconvw, jnp.sum(gmeta, axis=0)]
    gpacked, goffs = _pack_rows(small_g, D)
    gsum = _exchange_small(gpacked, True, "ar_small", after=prev)
    (grads["ln1_g"], grads["ln1_b"], grads["ln2_g"], grads["ln2_b"], grads["ln3_g"], grads["ln3_b"],
     grads["pool_scale"], g_conv_all, g_meta_all) = _unpack_rows(gsum, goffs, [p.shape for p in small_g])
    grads["meta_tokens"] = lax.dynamic_slice_in_dim(g_meta_all, q * (D // 4), D // 4, axis=1)
    grads["conv_w"] = lax.dynamic_slice_in_dim(g_conv_all, q * (CC // 4), CC // 4, axis=1)[None]
    finish(last, gsum)

    small = [n for n in names if n not in big]
    width = 4 * LANES
    pw, poffs = _pack_rows([ws[n] for n in small], width)
    pg, _ = _pack_rows([grads[n] for n in small], width)
    pm, _ = _pack_rows([ms[n] for n in small], width)
    pv, _ = _pack_rows([vs[n] for n in small], width)
    sd, sm, sv = _adamw(pw, pg, pm, pv, "adamw_small")
    shapes = [ws[n].shape for n in small]
    for n, d_, m_, v_ in zip(small, _unpack_rows(sd, poffs, shapes), _unpack_rows(sm, poffs, shapes),
                             _unpack_rows(sv, poffs, shapes)):
        delta[n], new_m[n], new_v[n] = d_, m_, v_

    return (loss, grad_x, *[grads[n] for n in names], *[delta[n] for n in names], *[new_m[n] for n in names],
            *[new_v[n] for n in names])
```

```python
import jax
import jax.numpy as jnp
from jax import lax
from jax.experimental import pallas as pl
from jax.experimental.pallas import tpu as pltpu

F32 = jnp.float32
BF16 = jnp.bfloat16
MESH = pl.DeviceIdType.MESH
ALL_AXES = ("x", "y", "c")

N_META = 16
POOL_WINDOWS = (2, 4, 8, 16)
LN_EPS = 1e-5
ALPHA = 2.0 ** 0.25
ADAM_LR, ADAM_B1, ADAM_B2, ADAM_EPS, ADAM_WD, ADAM_STEP = 0.001, 0.9, 0.999, 1e-08, 0.01, 10

VMEM_LIMIT_V7X = 60 * 1024 * 1024
LANES = 128
ELEMENTWISE_BLOCK_BYTES = 2 * 1024 * 1024


def _params(**kw):
    return pltpu.CompilerParams(vmem_limit_bytes=VMEM_LIMIT_V7X, **kw)


def _tile(n, prefs):
    for t in prefs:
        if t <= n and n % t == 0:
            return t
    return n


def _rows_tile(rows, cols, mult=16):
    cap = max(mult, ELEMENTWISE_BLOCK_BYTES // (4 * cols))
    best = None
    for t in range(mult, min(rows, cap) + 1, mult):
        if rows % t == 0:
            best = t
    return best if best is not None else rows


def _matmul(a, b, *, mode, tm, tn, tk, out_dtype, name, a_outer=True, scale=1.0, after=None, k_off=0, nk=None,
            addend=None, m_range=None, n_range=None):
    if mode == "tn":
        K, M = a.shape
    else:
        M, K = a.shape
    N = b.shape[0] if mode == "nt" else b.shape[1]
    assert M % tm == 0 and N % tn == 0 and K % tk == 0, (name, M, N, K, tm, tn, tk)
    nk = K // tk if nk is None else nk
    m0, mt = m_range if m_range is not None else (lambda: 0, M // tm)
    n0, nt = n_range if n_range is not None else (lambda: 0, N // tn)
    if a_outer:
        grid = (mt, nt, nk)
        ij = lambda p, q: (p, q)
    else:
        grid = (nt, mt, nk)
        ij = lambda p, q: (q, p)

    if mode == "tn":
        a_spec = pl.BlockSpec((tk, tm), lambda p, q, k: (k + k_off, ij(p, q)[0] + m0()))
        dims = (((0,), (0,)), ((), ()))
    else:
        a_spec = pl.BlockSpec((tm, tk), lambda p, q, k: (ij(p, q)[0] + m0(), k + k_off))
        dims = (((1,), (1,)), ((), ())) if mode == "nt" else (((1,), (0,)), ((), ()))
    if mode == "nt":
        b_spec = pl.BlockSpec((tn, tk), lambda p, q, k: (ij(p, q)[1] + n0(), k + k_off))
    else:
        b_spec = pl.BlockSpec((tk, tn), lambda p, q, k: (k + k_off, ij(p, q)[1] + n0()))
    o_spec = pl.BlockSpec((tm, tn), lambda p, q, k: ij(p, q))

    extra, extra_specs = [], []
    if addend is not None:
        extra.append(addend)
        extra_specs.append(o_spec)
    if after is not None:
        extra.append(after)
        extra_specs.append(pl.BlockSpec(memory_space=pl.ANY))
    n_in = 2 + len(extra)

    def fin(acc, refs):
        if scale != 1.0:
            acc = acc * scale
        if addend is not None:
            acc = acc + refs[2][...].astype(F32)
        return acc.astype(out_dtype)

    if nk == 1:
        def body(*refs):
            a_ref, b_ref, o_ref = refs[0], refs[1], refs[n_in]
            o_ref[...] = fin(lax.dot_general(a_ref[...], b_ref[...], dims, preferred_element_type=F32), refs)
        scratch = []
    else:
        def body(*refs):
            a_ref, b_ref, o_ref, acc_ref = refs[0], refs[1], refs[n_in], refs[n_in + 1]
            k = pl.program_id(2)
            prod = lax.dot_general(a_ref[...], b_ref[...], dims, preferred_element_type=F32)

            @pl.when(k == 0)
            def _():
                acc_ref[...] = prod

            @pl.when(k > 0)
            def _():
                acc_ref[...] += prod

            @pl.when(k == nk - 1)
            def _():
                o_ref[...] = fin(acc_ref[...], refs)
        scratch = [pltpu.VMEM((tm, tn), F32)]

    return pl.pallas_call(
        body, name=name, grid=grid, in_specs=[a_spec, b_spec] + extra_specs, out_specs=o_spec,
        out_shape=jax.ShapeDtypeStruct((mt * tm, nt * tn), out_dtype), scratch_shapes=scratch,
        compiler_params=_params(),
    )(a, b, *extra)


TM_PREFS = (1376, 688, 512, 256, 128, 64)


def _mm_nn(a, b, out_dtype, name, after=None):
    M, K = a.shape
    N = b.shape[1]
    big_k = K * 2 * 1376 > 12 * 1024 * 1024
    tm = _tile(M, (688,) + TM_PREFS[2:]) if big_k else _tile(M, TM_PREFS)
    tn = _tile(N, (256, 128)) if big_k else _tile(N, (512, 256, 128))
    return _matmul(a, b, mode="nn", tm=tm, tn=tn, tk=K, out_dtype=out_dtype, name=name, after=after)


def _mm_nt(a, b, out_dtype, name, scale=1.0, after=None, mid=None):
    M, K = a.shape
    N = b.shape[0]
    mib = 1024 * 1024
    parts = 1
    while K // parts * 2 * 688 > 16 * mib and K % (2 * parts * LANES) == 0:
        parts *= 2
    kp = K // parts
    tm = _tile(M, TM_PREFS) if kp * 2 * 1376 <= 12 * mib else _tile(M, TM_PREFS[1:])
    tn = _tile(N, (512, 256, 128))
    if 4 * kp * (tm + tn) > 44 * mib:
        tn = _tile(N, (256, 128))
    acc = None
    for p in range(parts):
        last = p == parts - 1
        acc = _matmul(a, b, mode="nt", tm=tm, tn=tn, tk=kp, k_off=p, nk=1, out_dtype=out_dtype if last else F32,
                      name=name, scale=scale, after=after, addend=acc)
        after = mid(acc) if mid is not None and p == 0 and not last else None
    if mid is not None and parts == 1:
        mid(acc)
    return acc


def _mm_tn(a, b, name, after=None, sh=None, which=None, addend=None):
    T, M = a.shape
    N = b.shape[1]
    if M % 2048 == 0:
        tm, tn, a_outer = 2048, _tile(N, (512, 256, 128)), True
    else:
        tm, tn, a_outer = _tile(M, (256, 128)), _tile(N, (2048, 1024, 512, 256, 128)), False
    m_range = n_range = None
    if which is not None:
        owner = _core if which == "mine" else (lambda: 1 - _core())
        if sh.col_halves:
            tn = min(tn, sh.hc)
            n_range = (lambda: owner() * (sh.hc // tn), sh.hc // tn)
        else:
            tm = min(tm, sh.h)
            m_range = (lambda: owner() * (sh.h // tm), sh.h // tm)
    return _matmul(a, b, mode="tn", tm=tm, tn=tn, tk=T, out_dtype=BF16, name=name, a_outer=a_outer, after=after,
                   m_range=m_range, n_range=n_range, addend=addend)


def _call(body, args, after, *, in_specs, **kw):
    if after is None:
        return pl.pallas_call(body, in_specs=in_specs, **kw)(*args)
    n = len(args)

    def ordered(*refs):
        return body(*refs[:n], *refs[n + 1:])

    return pl.pallas_call(ordered, in_specs=list(in_specs) + [pl.BlockSpec(memory_space=pl.ANY)], **kw)(*args, after)


def _silu_mul(gu, name, after=None):
    T, F2 = gu.shape
    F = F2 // 2
    tm = _rows_tile(T, F)

    def body(g_ref, u_ref, o_ref):
        g = g_ref[...].astype(F32)
        o_ref[...] = (g * jax.nn.sigmoid(g) * u_ref[...].astype(F32)).astype(BF16)

    return _call(
        body, (gu, gu), after, name=name, grid=(T // tm,),
        in_specs=[pl.BlockSpec((tm, F), lambda i: (i, 0)), pl.BlockSpec((tm, F), lambda i: (i, 1))],
        out_specs=pl.BlockSpec((tm, F), lambda i: (i, 0)),
        out_shape=jax.ShapeDtypeStruct((T, F), BF16), compiler_params=_params(),
    )


def _swiglu_bwd(da, gu, name, after=None):
    T, F2 = gu.shape
    F = F2 // 2
    tm = _rows_tile(T, F)

    def body(da_ref, g_ref, u_ref, o_ref):
        g = g_ref[...].astype(F32)
        da_ = da_ref[...].astype(F32)
        s = jax.nn.sigmoid(g)
        o_ref[:, :F] = (da_ * u_ref[...].astype(F32) * (s * (1.0 + g * (1.0 - s)))).astype(BF16)
        o_ref[:, F:] = (da_ * (g * s)).astype(BF16)

    return _call(
        body, (da, gu, gu), after, name=name, grid=(T // tm,),
        in_specs=[pl.BlockSpec((tm, F), lambda i: (i, 0)), pl.BlockSpec((tm, F), lambda i: (i, 0)),
                  pl.BlockSpec((tm, F), lambda i: (i, 1))],
        out_specs=pl.BlockSpec((tm, F2), lambda i: (i, 0)),
        out_shape=jax.ShapeDtypeStruct((T, F2), BF16), compiler_params=_params(),
    )


def _ln_stats(r):
    mu = jnp.mean(r, axis=-1, keepdims=True)
    xc = r - mu
    var = jnp.mean(xc * xc, axis=-1, keepdims=True)
    rstd = lax.rsqrt(var + LN_EPS)
    return xc * rstd, rstd


def _ln_bwd_math(dh, xhat, rstd, g):
    dxh = dh * g
    m1 = jnp.mean(dxh, axis=-1, keepdims=True)
    m2 = jnp.mean(dxh * xhat, axis=-1, keepdims=True)
    return rstd * (dxh - m1 - xhat * m2)


def _ln_fwd(hprev, f, g, b, fscale, name, after=None):
    T, D = hprev.shape
    tm = _rows_tile(T, D)

    def body(hp_ref, f_ref, g_ref, b_ref, h_ref, hb_ref, xh_ref, rs_ref):
        r = ALPHA * hp_ref[...] + fscale * f_ref[...]
        xhat, rstd = _ln_stats(r)
        h = xhat * g_ref[...] + b_ref[...]
        h_ref[...] = h
        hb_ref[...] = h.astype(BF16)
        xh_ref[...] = xhat
        rs_ref[...] = rstd

    row = pl.BlockSpec((tm, D), lambda i: (i, 0))
    vec = pl.BlockSpec((1, D), lambda i: (0, 0))
    col = pl.BlockSpec((tm, 1), lambda i: (i, 0))
    return _call(
        body, (hprev, f, g, b), after, name=name, grid=(T // tm,), in_specs=[row, row, vec, vec],
        out_specs=[row, row, row, col],
        out_shape=[jax.ShapeDtypeStruct((T, D), F32), jax.ShapeDtypeStruct((T, D), BF16),
                   jax.ShapeDtypeStruct((T, D), F32), jax.ShapeDtypeStruct((T, 1), F32)],
        compiler_params=_params(),
    )


def _ln_bwd(dr_next, dh_branch, xhat, rstd, g, fscale, name, after=None):
    T, D = xhat.shape
    tm = _rows_tile(T, D)

    def body(dn_ref, db_ref, xh_ref, rs_ref, g_ref, dr_ref, drb_ref, gs_ref, bs_ref):
        i = pl.program_id(0)
        dh = ALPHA * dn_ref[...] + db_ref[...]
        xhat_ = xh_ref[...]
        dr = _ln_bwd_math(dh, xhat_, rs_ref[...], g_ref[...])
        dr_ref[...] = dr
        drb_ref[...] = (fscale * dr).astype(BF16)

        @pl.when(i == 0)
        def _():
            gs_ref[...] = jnp.zeros_like(gs_ref)
            bs_ref[...] = jnp.zeros_like(bs_ref)

        gs_ref[...] += jnp.sum(dh * xhat_, axis=0, keepdims=True)
        bs_ref[...] += jnp.sum(dh, axis=0, keepdims=True)

    row = pl.BlockSpec((tm, D), lambda i: (i, 0))
    vec = pl.BlockSpec((1, D), lambda i: (0, 0))
    col = pl.BlockSpec((tm, 1), lambda i: (i, 0))
    return _call(
        body, (dr_next, dh_branch, xhat, rstd, g), after, name=name, grid=(T // tm,),
        in_specs=[row, row, row, col, vec], out_specs=[row, row, vec, vec],
        out_shape=[jax.ShapeDtypeStruct((T, D), F32), jax.ShapeDtypeStruct((T, D), BF16),
                   jax.ShapeDtypeStruct((1, D), F32), jax.ShapeDtypeStruct((1, D), F32)],
        compiler_params=_params(),
    )


def _ln3_loss(hprev, f, target, mask, g, b, name):
    T, D = hprev.shape
    tm = _rows_tile(T, D)

    def body(hp_ref, f_ref, t_ref, m_ref, g_ref, b_ref, dr_ref, drb_ref, gs_ref, bs_ref, ls_ref):
        i = pl.program_id(0)
        r = ALPHA * hp_ref[...] + 0.5 * f_ref[...]
        xhat, rstd = _ln_stats(r)
        g_ = g_ref[...]
        y = xhat * g_ + b_ref[...]
        err = (y - t_ref[...]) * m_ref[...]
        dy = err * (1.0 / D)
        dr = _ln_bwd_math(dy, xhat, rstd, g_)
        dr_ref[...] = dr
        drb_ref[...] = (0.5 * dr).astype(BF16)

        @pl.when(i == 0)
        def _():
            gs_ref[...] = jnp.zeros_like(gs_ref)
            bs_ref[...] = jnp.zeros_like(bs_ref)
            ls_ref[...] = jnp.zeros_like(ls_ref)

        gs_ref[...] += jnp.sum(dy * xhat, axis=0, keepdims=True)
        bs_ref[...] += jnp.sum(dy, axis=0, keepdims=True)
        ls_ref[...] += jnp.sum(err * err, axis=0, keepdims=True)

    row = pl.BlockSpec((tm, D), lambda i: (i, 0))
    vec = pl.BlockSpec((1, D), lambda i: (0, 0))
    col = pl.BlockSpec((tm, 1), lambda i: (i, 0))
    return pl.pallas_call(
        body, name=name, grid=(T // tm,), in_specs=[row, row, row, col, vec, vec],
        out_specs=[row, row, vec, vec, vec],
        out_shape=[jax.ShapeDtypeStruct((T, D), F32), jax.ShapeDtypeStruct((T, D), BF16),
                   jax.ShapeDtypeStruct((1, D), F32), jax.ShapeDtypeStruct((1, D), F32),
                   jax.ShapeDtypeStruct((1, D), F32)],
        compiler_params=_params(),
    )(hprev, f, target, mask, g, b)


def _down(x, k):
    rows = lax.broadcasted_iota(jnp.int32, x.shape, 0)
    return jnp.where(rows >= k, pltpu.roll(x, k, axis=0), 0.0)


def _up(x, k):
    n = x.shape[0]
    rows = lax.broadcasted_iota(jnp.int32, x.shape, 0)
    return jnp.where(rows < n - k, pltpu.roll(x, n - k, axis=0), 0.0)


def _conv_fwd(u, conv_w, L, CC, name, after=None):
    T = u.shape[0]
    tc = _tile(CC, (256, 128))
    n = CC // tc

    def body(b_ref, c_ref, x_ref, w_ref, o_ref):
        cx = c_ref[...] * x_ref[...]
        w = w_ref[...]
        conv = w[0:1] * _down(cx, 2) + w[1:2] * _down(cx, 1) + w[2:3] * cx
        o_ref[...] = (b_ref[...] * conv).astype(BF16)

    blk = lambda off: pl.BlockSpec((L, tc), lambda s, j: (s, j + off * n))
    return _call(
        body, (u, u, u, conv_w), after, name=name, grid=(T // L, n),
        in_specs=[blk(0), blk(1), blk(2), pl.BlockSpec((3, tc), lambda s, j: (0, j))],
        out_specs=pl.BlockSpec((L, tc), lambda s, j: (s, j)),
        out_shape=jax.ShapeDtypeStruct((T, CC), BF16), compiler_params=_params(),
    )


def _conv_bwd(u, dy, conv_w, L, CC, name, after=None):
    T = u.shape[0]
    tc = _tile(CC, (256, 128))
    n = CC // tc

    def body(b_ref, c_ref, x_ref, dy_ref, w_ref, db_ref, dc_ref, dx_ref, dw_ref):
        s = pl.program_id(1)
        c_, x_ = c_ref[...], x_ref[...]
        cx = c_ * x_
        w = w_ref[...]
        cx1, cx2 = _down(cx, 1), _down(cx, 2)
        conv = w[0:1] * cx2 + w[1:2] * cx1 + w[2:3] * cx
        dy_ = dy_ref[...]
        db_ref[...] = (dy_ * conv).astype(BF16)
        dconv = dy_ * b_ref[...]
        dcx = w[2:3] * dconv + w[1:2] * _up(dconv, 1) + w[0:1] * _up(dconv, 2)
        dc_ref[...] = (dcx * x_).astype(BF16)
        dx_ref[...] = (dcx * c_).astype(BF16)

        @pl.when(s == 0)
        def _():
            dw_ref[...] = jnp.zeros_like(dw_ref)

        dw_ref[0:1, :] += jnp.sum(dconv * cx2, axis=0, keepdims=True)
        dw_ref[1:2, :] += jnp.sum(dconv * cx1, axis=0, keepdims=True)
        dw_ref[2:3, :] += jnp.sum(dconv * cx, axis=0, keepdims=True)

    blk = lambda off: pl.BlockSpec((L, tc), lambda j, s: (s, j + off * n))
    out = pl.BlockSpec((L, tc), lambda j, s: (s, j))
    act = jax.ShapeDtypeStruct((T, CC), BF16)
    return _call(
        body, (u, u, u, dy, conv_w), after, name=name, grid=(n, T // L),
        in_specs=[blk(0), blk(1), blk(2), blk(0), pl.BlockSpec((3, tc), lambda j, s: (0, j))],
        out_specs=[out, out, out, pl.BlockSpec((3, tc), lambda j, s: (0, j))],
        out_shape=[act, act, act, jax.ShapeDtypeStruct((3, CC), F32)], compiler_params=_params(),
    )


def _pool_counts(shape, g):
    rows = lax.broadcasted_iota(jnp.int32, shape, 0)
    win = jnp.left_shift(jnp.int32(POOL_WINDOWS[0]), g)
    return jnp.minimum(rows + 1, win).astype(F32)


def _pick_group(g, vals):
    out = vals[-1]
    for i in range(len(vals) - 2, -1, -1):
        out = jnp.where(g == i, vals[i], out)
    return out


def _pool_fwd(u, pool_w, pool_scale, L, CC, PG, name):
    T = u.shape[0]
    G = len(POOL_WINDOWS)
    off = 3 * CC // PG

    def body(z_ref, w_ref, sc_ref, y_ref, d_ref):
        g = pl.program_id(1)
        z = z_ref[...]
        s1 = z + _down(z, 1)
        s2 = s1 + _down(s1, 2)
        s3 = s2 + _down(s2, 4)
        s4 = s3 + _down(s3, 8)
        pooled = _pick_group(g, [s1, s2, s3, s4]) / _pool_counts(z.shape, g)
        d = (pooled - z).astype(BF16)
        d_ref[...] = d
        q = jnp.dot(d, w_ref[...], preferred_element_type=F32)
        y_ref[...] = (q * sc_ref[...]).astype(BF16)

    blk = pl.BlockSpec((L, PG), lambda s, g: (s, g))
    act = jax.ShapeDtypeStruct((T, G * PG), BF16)
    return pl.pallas_call(
        body, name=name, grid=(T // L, G),
        in_specs=[pl.BlockSpec((L, PG), lambda s, g: (s, off + g)), pl.BlockSpec((None, PG, PG), lambda s, g: (g, 0, 0)),
                  pl.BlockSpec((1, PG), lambda s, g: (0, g))],
        out_specs=[blk, blk], out_shape=[act, act], compiler_params=_params(),
    )(u, pool_w, pool_scale)


def _pool_bwd(d, dy, pool_w, pool_scale, L, CC, PG, name):
    T = d.shape[0]
    G = len(POOL_WINDOWS)
    off = CC // PG
    PGs = PG // 4

    def body(d_ref, dy_ref, w_ref, sc_ref, dz_ref, dw_ref, dsc_ref):
        g = pl.program_id(0)
        s = pl.program_id(1)
        d_ = d_ref[...]
        w = w_ref[...]
        dy_ = dy_ref[...]
        q = jnp.dot(d_, w, preferred_element_type=F32)
        dq = (dy_ * sc_ref[...]).astype(BF16)
        dd = lax.dot_general(dq, w, (((1,), (1,)), ((), ())), preferred_element_type=F32)
        e = dd / _pool_counts(dd.shape, g)
        a1 = e + _up(e, 1)
        a2 = a1 + _up(a1, 2)
        a3 = a2 + _up(a2, 4)
        a4 = a3 + _up(a3, 8)
        dz_ref[...] = (_pick_group(g, [a1, a2, a3, a4]) - dd).astype(BF16)

        @pl.when(s == 0)
        def _():
            dw_ref[...] = jnp.zeros_like(dw_ref)
            dsc_ref[...] = jnp.zeros_like(dsc_ref)

        dw = lax.dot_general(d_, dq, (((0,), (0,)), ((), ())), preferred_element_type=F32)
        for k in range(4):
            dw_ref[k] += dw[k * PGs:(k + 1) * PGs, :]
        dsc_ref[...] += jnp.sum(dy_ * q, axis=0, keepdims=True)

    return pl.pallas_call(
        body, name=name, grid=(G, T // L),
        in_specs=[pl.BlockSpec((L, PG), lambda g, s: (s, g)), pl.BlockSpec((L, PG), lambda g, s: (s, off + g)),
                  pl.BlockSpec((None, PG, PG), lambda g, s: (g, 0, 0)), pl.BlockSpec((1, PG), lambda g, s: (0, g))],
        out_specs=[pl.BlockSpec((L, PG), lambda g, s: (s, g)),
                   pl.BlockSpec((4, None, PGs, PG), lambda g, s: (0, g, 0, 0)),
                   pl.BlockSpec((1, PG), lambda g, s: (0, g))],
        out_shape=[jax.ShapeDtypeStruct((T, G * PG), BF16), jax.ShapeDtypeStruct((4, G, PGs, PG), F32),
                   jax.ShapeDtypeStruct((1, G * PG), F32)],
        compiler_params=_params(),
    )(d, dy, pool_w, pool_scale)


def _input_grad(dr1, dh_branch, L, name, after=None):
    T, D = dr1.shape
    nseq = T // L
    tc = _tile(D, (512, 256, 128))

    def body(a_ref, b_ref, gx_ref, gm_ref):
        dh = ALPHA * a_ref[...] + b_ref[...]
        gm_ref[...] = dh[:N_META]
        gx_ref[...] = dh[N_META:]

    blk = pl.BlockSpec((L, tc), lambda s, j: (s, j))
    return _call(
        body, (dr1, dh_branch), after, name=name, grid=(nseq, D // tc), in_specs=[blk, blk],
        out_specs=[pl.BlockSpec((None, L - N_META, tc), lambda s, j: (s, 0, j)),
                   pl.BlockSpec((None, N_META, tc), lambda s, j: (s, 0, j))],
        out_shape=[jax.ShapeDtypeStruct((nseq, L - N_META, D), F32), jax.ShapeDtypeStruct((nseq, N_META, D), F32)],
        compiler_params=_params(),
    )


def _chip():
    return 2 * lax.axis_index("x") + lax.axis_index("y")


def _core():
    return lax.axis_index("c")


def _cast_into_full(w, sh, name):
    tr = _rows_tile(sh.srows, sh.scols)
    nb = sh.srows // tr

    def body(w_ref, o_ref):
        o_ref[...] = w_ref[...].astype(BF16)

    if sh.axis == 0:
        out = pl.BlockSpec((tr, sh.scols), lambda i: (_chip() * nb + i, 0))
    else:
        out = pl.BlockSpec((tr, sh.scols), lambda i: (i, _chip()))
    return pl.pallas_call(
        body, name=name, grid=(nb,), in_specs=[pl.BlockSpec((tr, sh.scols), lambda i: (i, 0))], out_specs=out,
        out_shape=jax.ShapeDtypeStruct((sh.rows, sh.cols), BF16), compiler_params=_params(),
    )(w)


def _add_halves(dw, theirs, sh, name):
    R, C = sh.half_all_shape
    tr = _rows_tile(sh.h, C)
    nh, ns = sh.h // tr, sh.srows // tr

    def body(a_ref, b_ref, o_ref):
        o_ref[...] = (a_ref[...].astype(F32) + b_ref[...].astype(F32)).astype(BF16)

    blk = pl.BlockSpec((tr, C), lambda i: (i, 0))
    mine = pl.BlockSpec((tr, C), lambda i: ((i // nh) * ns + _core() * nh + i % nh, 0))
    return pl.pallas_call(body, name=name, grid=(R // tr,), in_specs=[mine, blk], out_specs=blk,
                          out_shape=jax.ShapeDtypeStruct((R, C), BF16), compiler_params=_params())(dw, theirs)


def _sum_parts(part, recv, sh, name):
    h, C = sh.half_shape
    tr = _rows_tile(h, C)
    nh = h // tr

    def body(o_ref, r_ref, out_ref):
        acc = o_ref[...].astype(F32)
        for k in range(3):
            acc = acc + r_ref[k].astype(F32)
        out_ref[...] = acc

    if sh.axis == 0:
        own = pl.BlockSpec((tr, C), lambda i: (_chip() * nh + i, 0))
    else:
        own = pl.BlockSpec((tr, C), lambda i: (i, _chip()))
    if sh.col_halves:
        out = pl.BlockSpec((tr, C), lambda i: (i, _core()))
    else:
        out = pl.BlockSpec((tr, C), lambda i: (_core() * nh + i, 0))
    return pl.pallas_call(
        body, name=name, grid=(nh,), in_specs=[own, pl.BlockSpec((3, tr, C), lambda i: (0, i, 0))], out_specs=out,
        out_shape=jax.ShapeDtypeStruct((sh.srows, sh.scols), F32), compiler_params=_params(),
    )(part, recv)


def _adamw(w, g, m, v, name, copy_grad=False):
    R, C = w.shape
    tr = _rows_tile(R, C, mult=8)

    def body(w_ref, g_ref, m_ref, v_ref, d_ref, nm_ref, nv_ref, *g_out):
        g_ = g_ref[...]
        for o in g_out:
            o[...] = g_
        m_ = ADAM_B1 * m_ref[...] + (1.0 - ADAM_B1) * g_
        v_ = ADAM_B2 * v_ref[...] + (1.0 - ADAM_B2) * (g_ * g_)
        m_hat = m_ / (1.0 - ADAM_B1 ** ADAM_STEP)
        v_hat = v_ / (1.0 - ADAM_B2 ** ADAM_STEP)
        d_ref[...] = -ADAM_LR * (m_hat / (jnp.sqrt(v_hat) + ADAM_EPS) + ADAM_WD * w_ref[...])
        nm_ref[...] = m_
        nv_ref[...] = v_

    blk = pl.BlockSpec((tr, C), lambda i: (i, 0))
    shp = jax.ShapeDtypeStruct((R, C), F32)
    n_out = 4 if copy_grad else 3
    return pl.pallas_call(body, name=name, grid=(R // tr,), in_specs=[blk] * 4, out_specs=[blk] * n_out,
                          out_shape=[shp] * n_out, compiler_params=_params())(w, g, m, v)


def _pos():
    return lax.axis_index("x"), lax.axis_index("y"), lax.axis_index("c")


def _other_chips(x, y):
    return [(1 - x, y), (x, 1 - y), (1 - x, 1 - y)]


class _Sharded:
    def __init__(self, full_shape, axis, col_halves=False):
        self.rows, self.cols = full_shape
        self.axis = axis
        self.srows = self.rows // 4 if axis == 0 else self.rows
        self.scols = self.cols if axis == 0 else self.cols // 4
        assert self.srows % 2 == 0
        self.h = self.srows // 2
        self.col_halves = col_halves
        assert not (col_halves and axis == 1)
        self.hc = self.cols // 2
        if col_halves:
            self.half_all_shape, self.half_shape = (self.rows, self.hc), (self.srows, self.hc)
        else:
            self.half_all_shape = (4 * self.h, self.cols) if axis == 0 else (self.h, self.cols)
            self.half_shape = (self.h, self.scols)

    def shard(self, ref, q):
        if self.axis == 0:
            return ref.at[pl.ds(q * self.srows, self.srows), :]
        return ref.at[:, pl.ds(pl.multiple_of(q * self.scols, LANES), self.scols)]

    def half(self, ref, q, c):
        if self.axis == 0:
            return ref.at[pl.ds(q * self.srows + c * self.h, self.h), :]
        return ref.at[pl.ds(c * self.h, self.h), pl.ds(pl.multiple_of(q * self.scols, LANES), self.scols)]

    def half_in_stack(self, ref, q):
        if self.col_halves:
            return ref.at[pl.ds(q * self.srows, self.srows), :]
        if self.axis == 0:
            return ref.at[pl.ds(q * self.h, self.h), :]
        return ref.at[:, pl.ds(pl.multiple_of(q * self.scols, LANES), self.scols)]


ANY = pl.BlockSpec(memory_space=pl.ANY)
DMA_ROW_ALIGN = 16
CHUNK_COUNTS = (16, 8, 43, 4, 2)


def _row_chunks(rows):
    n = 1
    if rows % DMA_ROW_ALIGN == 0:
        n = _tile(rows // DMA_ROW_ALIGN, CHUNK_COUNTS)
        n = n if n in CHUNK_COUNTS else 1
    size = rows // n
    return [(k * size, size) for k in range(n)]


class _Pieces:
    def __init__(self, src, dst, send_sem, recv_sem, to):
        self.args = (src, dst, send_sem, recv_sem, to)

    def _copy(self, rows=None):
        src, dst, send_sem, recv_sem, to = self.args
        if rows is not None:
            src, dst = src.at[pl.ds(*rows), :], dst.at[pl.ds(*rows), :]
        return pltpu.make_async_remote_copy(src_ref=src, dst_ref=dst, send_sem=send_sem, recv_sem=recv_sem,
                                            device_id=to, device_id_type=MESH)

    def start(self):
        for rows in _row_chunks(self.args[0].shape[0]):
            self._copy(rows).start()

    def wait_send(self):
        self._copy().wait_send()

    def wait_recv(self):
        self._copy().wait_recv()

    def wait(self):
        self._copy().wait()


HBM = pl.BlockSpec(memory_space=pltpu.HBM)
SEM = pl.BlockSpec(memory_space=pltpu.SEMAPHORE)
SPLIT_COPY = pltpu.CompilerParams(has_side_effects=pltpu.SideEffectType.DATAFLOW_SIDE_EFFECTING)
TOKEN = jax.ShapeDtypeStruct((8, LANES), F32)


def _in_hbm(a):
    return pltpu.with_memory_space_constraint(a, pltpu.HBM)


def _split_start(arrays, plan, n, name):
    na = len(arrays)

    def body(*refs):
        send_sems, recv_sems, token = refs[na], refs[na + 1], refs[-1]
        x, y, c = _pos()
        for j, (src, dst, to, _) in enumerate(plan(refs[:na], x, y, c)):
            _Pieces(src, dst, send_sems.at[j], recv_sems.at[j], to).start()
        token[...] = jnp.zeros_like(token)

    outs = pl.pallas_call(
        body, name=name, in_specs=(HBM,) * na,
        out_shape=(pltpu.SemaphoreType.DMA((n,)), pltpu.SemaphoreType.DMA((n,)),
                   *[pltpu.HBM(a.shape, a.dtype) for a in arrays], TOKEN),
        out_specs=(SEM, SEM) + (HBM,) * na + (pl.BlockSpec(memory_space=pltpu.VMEM),),
        input_output_aliases={i: 2 + i for i in range(na)}, compiler_params=SPLIT_COPY,
    )(*[_in_hbm(a) for a in arrays])
    return (outs[0], outs[1], list(outs[2:2 + na])), outs[-1]


def _split_wait(state, after, plan, name):
    send_sems, recv_sems, arrays = state
    na = len(arrays)

    def body(*refs):
        send_sems, recv_sems = refs[na], refs[na + 1]
        x, y, c = _pos()
        for j, (src, _, to, landing) in enumerate(plan(refs[:na], x, y, c)):
            cp = pltpu.make_async_remote_copy(src_ref=src, dst_ref=landing, send_sem=send_sems.at[j],
                                              recv_sem=recv_sems.at[j], device_id=to, device_id_type=MESH)
            cp.wait_send()
            cp.wait_recv()

    outs = pl.pallas_call(
        body, name=name, in_specs=(HBM,) * na + (SEM, SEM, ANY),
        out_shape=tuple(pltpu.HBM(a.shape, a.dtype) for a in arrays), out_specs=(HBM,) * na,
        input_output_aliases={i: i for i in range(na)}, compiler_params=SPLIT_COPY,
    )(*arrays, send_sems, recv_sems, after)
    return list(outs)


def _plan_gather_ici(sh):
    def plan(refs, x, y, c):
        (full,) = refs
        mine = sh.half(full, 2 * x + y, c)
        return [(mine, mine, (cx, cy, c), sh.half(full, 2 * cx + cy, c)) for cx, cy in _other_chips(x, y)]
    return plan


def _plan_gather_d2d(sh):
    def plan(refs, x, y, c):
        (full,) = refs
        out = []
        for cx, cy in _other_chips(x, y):
            landed = sh.half(full, 2 * cx + cy, c)
            out.append((landed, landed, (x, y, 1 - c), sh.half(full, 2 * cx + cy, 1 - c)))
        return out
    return plan


def _plan_rs_sibling(sh):
    def plan(refs, x, y, c):
        dw, theirs = refs
        if sh.axis == 1:
            return [(dw.at[pl.ds((1 - c) * sh.h, sh.h), :], theirs, (x, y, 1 - c), theirs)]
        out = []
        for k in range(4):
            give = dw.at[pl.ds(k * sh.srows + (1 - c) * sh.h, sh.h), :]
            theirs_k = theirs.at[pl.ds(k * sh.h, sh.h), :]
            out.append((give, theirs_k, (x, y, 1 - c), theirs_k))
        return out
    return plan


def _plan_rs_ici(sh):
    def plan(refs, x, y, c):
        part, land = refs
        return [(sh.half_in_stack(part, 2 * cx + cy), land.at[j], (cx, cy, c), land.at[j])
                for j, (cx, cy) in enumerate(_other_chips(x, y))]
    return plan


def _plan_share(sh):
    def plan(refs, x, y, c):
        (g,) = refs
        if sh.col_halves:
            mine = g.at[:, pl.ds(pl.multiple_of(c * sh.hc, LANES), sh.hc)]
            theirs = g.at[:, pl.ds(pl.multiple_of((1 - c) * sh.hc, LANES), sh.hc)]
        else:
            mine, theirs = g.at[pl.ds(c * sh.h, sh.h), :], g.at[pl.ds((1 - c) * sh.h, sh.h), :]
        return [(mine, mine, (x, y, 1 - c), theirs)]
    return plan


def _plan_to_sibling(refs, x, y, c):
    mine, theirs = refs
    return [(mine, theirs, (x, y, 1 - c), theirs)]


def _exchange_small(v, reduce, name, after=None):
    R, C = v.shape

    def body(v_ref, o_ref, buf, send_sems, recv_sems):
        x, y, c = _pos()
        me = 4 * x + 2 * y + c
        slots = buf if reduce else o_ref
        slots[me] = v_ref[...]
        cps = []
        for k in range(1, 8):
            px = 1 - x if k & 4 else x
            py = 1 - y if k & 2 else y
            pc = 1 - c if k & 1 else c
            cps.append((_Pieces(v_ref, slots.at[me], send_sems.at[k - 1], recv_sems.at[k - 1], (px, py, pc)),
                        4 * px + 2 * py + pc))
        for cp, _ in cps:
            cp.start()
        for k, (cp, peer) in enumerate(cps):
            pltpu.make_async_remote_copy(
                src_ref=v_ref, dst_ref=slots.at[peer], send_sem=send_sems.at[k], recv_sem=recv_sems.at[k],
                device_id=(x, y, c), device_id_type=MESH).wait_recv()
        if reduce:
            acc = buf[0]
            for d in range(1, 8):
                acc = acc + buf[d]
            o_ref[...] = acc
        for cp, _ in cps:
            cp.wait_send()

    vm = pl.BlockSpec(memory_space=pltpu.VMEM)
    out_shape = jax.ShapeDtypeStruct((R, C) if reduce else (8, R, C), v.dtype)
    scratch = [pltpu.VMEM((8, R, C) if reduce else (8, LANES), v.dtype),
               pltpu.SemaphoreType.DMA((7,)), pltpu.SemaphoreType.DMA((7,))]
    return _call(body, (v,), after, name=name, in_specs=[vm], out_specs=vm, out_shape=out_shape,
                 scratch_shapes=scratch, compiler_params=_params())


def _pack_rows(parts, width):
    rows, offs, at = [], [], 0
    for p in parts:
        flat = p.reshape(-1)
        n = 8 * -(-flat.shape[0] // (8 * width))
        flat = jnp.pad(flat, (0, n * width - flat.shape[0]))
        rows.append(flat.reshape(n, width))
        offs.append(at)
        at += n
    return jnp.concatenate(rows, axis=0), offs


def _unpack_rows(packed, offs, shapes):
    out = []
    for off, shp in zip(offs, shapes):
        size = 1
        for s in shp:
            size *= s
        n = -(-size // packed.shape[1])
        out.append(packed[off:off + n].reshape(-1)[:size].reshape(shp))
    return out


def kernel(x, meta_tokens, ffn1_w_gu, ffn1_w_down, ln1_g, ln1_b, w_in, conv_w, pool_w, pool_scale, w_out, ln2_g, ln2_b, ffn2_w_gu, ffn2_w_down, ln3_g, ln3_b, loss_target, m_meta_tokens, m_ffn1_w_gu, m_ffn1_w_down, m_ln1_g, m_ln1_b, m_w_in, m_conv_w, m_pool_w, m_pool_scale, m_w_out, m_ln2_g, m_ln2_b, m_ffn2_w_gu, m_ffn2_w_down, m_ln3_g, m_ln3_b, v_meta_tokens, v_ffn1_w_gu, v_ffn1_w_down, v_ln1_g, v_ln1_b, v_w_in, v_conv_w, v_pool_w, v_pool_scale, v_w_out, v_ln2_g, v_ln2_b, v_ffn2_w_gu, v_ffn2_w_down, v_ln3_g, v_ln3_b):
    nseq, S, D = x.shape
    L = S + N_META
    T = nseq * L
    F = ffn1_w_down.shape[1] * 4
    CC = conv_w.shape[2] * 4
    G, PGs, PG = pool_w.shape[1:]
    PC = G * PG
    IN = w_in.shape[2] * 4
    qx, qy, qc = _pos()
    q = 2 * qx + qy

    sh_gu = _Sharded((D, 2 * F), 1)
    sh_down = _Sharded((F, D), 0, col_halves=True)
    sh_in = _Sharded((D, IN), 1)
    sh_out = _Sharded((CC + PC, D), 0, col_halves=True)
    sh_pool = _Sharded((4 * G * PGs, PG), 0)

    gathers = {}
    tok = jnp.zeros((), F32)
    for tag, w, sh in (("gu1", ffn1_w_gu, sh_gu), ("d1", ffn1_w_down, sh_down), ("in", w_in, sh_in),
                       ("out", w_out, sh_out), ("gu2", ffn2_w_gu, sh_gu), ("d2", ffn2_w_down, sh_down)):
        full = _cast_into_full(w[0], sh, "cast_" + tag)
        state, t_ = _split_start([full], _plan_gather_ici(sh), 3, "ag_start_" + tag)
        gathers[tag] = (state, sh)
        tok = tok + t_[0, 0]

    def landed(tag, after):
        state, sh = gathers.pop(tag)
        (full,) = _split_wait(state, after, _plan_gather_ici(sh), "ag_wait_" + tag)
        state, t_ = _split_start([full], _plan_gather_d2d(sh), 3, "ag_pass_" + tag)
        gathers[tag] = (state, sh)
        return t_

    def arrive(tag, after):
        state, sh = gathers.pop(tag)
        return _split_wait(state, after, _plan_gather_d2d(sh), "ag_got_" + tag)[0]

    small_w = [meta_tokens, conv_w[0], pool_w[0]]
    packed, offs = _pack_rows(small_w, 4 * LANES)
    slots = _exchange_small(packed, False, "ag_small")
    per_chip = [_unpack_rows(slots[2 * k], offs, [p.shape for p in small_w]) for k in range(4)]
    meta_full = jnp.concatenate([p[0] for p in per_chip], axis=1)
    conv_full = jnp.concatenate([p[1] for p in per_chip], axis=1)
    poolw_full = jnp.concatenate([p[2] for p in per_chip], axis=1).astype(BF16)
    pscale = pool_scale

    h0 = jnp.concatenate([jnp.broadcast_to(meta_full[None], (nseq, N_META, D)), x], axis=1).reshape(T, D)
    h0b = (h0 + tok).astype(BF16)
    tgt = jnp.pad(loss_target, ((0, 0), (N_META, 0), (0, 0))).reshape(T, D)
    mask = (lax.broadcasted_iota(jnp.int32, (nseq, L, 1), 1) >= N_META).astype(F32).reshape(T, 1)

    wgu1 = arrive("gu1", landed("gu1", h0b))
    gu1 = _mm_nn(h0b, wgu1, BF16, "ffn_gu")
    a1 = _silu_mul(gu1, "silu_mul", after=landed("d1", gu1))
    wd1 = arrive("d1", a1)
    f1 = _mm_nn(a1, wd1, F32, "ffn_down")
    h1, h1b, xh1, rs1 = _ln_fwd(h0, f1, ln1_g, ln1_b, 0.5, "ln_fwd", after=landed("in", f1))
    win = arrive("in", h1b)

    u = _mm_nn(h1b, win, F32, "mix_in")
    yc = _conv_fwd(u, conv_full, L, CC, "conv_fwd", after=landed("out", u))
    yp, dpool = _pool_fwd(u, poolw_full, pscale, L, CC, PG, "pool_fwd")
    ymix = jnp.concatenate([yc, yp], axis=1)
    wout = arrive("out", ymix)
    o = _mm_nn(ymix, wout, F32, "mix_out")
    h2, h2b, xh2, rs2 = _ln_fwd(h1, o, ln2_g, ln2_b, 1.0, "ln_fwd", after=landed("gu2", o))
    wgu2 = arrive("gu2", h2b)

    gu2 = _mm_nn(h2b, wgu2, BF16, "ffn_gu")
    a2 = _silu_mul(gu2, "silu_mul", after=landed("d2", gu2))
    wd2 = arrive("d2", a2)
    f2 = _mm_nn(a2, wd2, F32, "ffn_down")
    dr3, dr3b, gs3, bs3, lsum = _ln3_loss(h2, f2, tgt, mask, ln3_g, ln3_b, "ln3_loss")
    loss = lax.psum(0.5 * jnp.sum(lsum) / D, ALL_AXES)

    sibs, icis, shares = {}, {}, {}

    def sib_begin(dw, sh, tag):
        theirs = lax.empty(sh.half_all_shape, dw.dtype)
        state, t_ = _split_start([dw, theirs], _plan_rs_sibling(sh), 4 if sh.axis == 0 else 1, "rs_sib_" + tag)
        sibs[tag] = (state, sh)
        return t_

    def ici_begin(tag, after):
        state, sh = sibs.pop(tag)
        dw, theirs = _split_wait(state, after, _plan_rs_sibling(sh), "rs_sibw_" + tag)
        part = _add_halves(dw, theirs, sh, "rs_add_" + tag)
        land = lax.empty((3,) + sh.half_shape, part.dtype)
        state, t_ = _split_start([part, land], _plan_rs_ici(sh), 3, "rs_start_" + tag)
        icis[tag] = (state, sh)
        return t_

    def share_begin(tag, after):
        state, sh = icis.pop(tag)
        part, recv = _split_wait(state, after, _plan_rs_ici(sh), "rs_wait_" + tag)
        g = _sum_parts(part, recv, sh, "rs_sum_" + tag)
        state, t_ = _split_start([g], _plan_share(sh), 1, "rs_share_" + tag)
        shares[tag] = (state, sh)
        return t_

    def share_end(tag, after):
        state, sh = shares.pop(tag)
        return _split_wait(state, after, _plan_share(sh), "rs_got_" + tag)[0]

    def dw_theirs(a, b, sh, tag, name, after=None):
        half = _mm_tn(a, b, name, after=after, sh=sh, which="theirs")
        state, t_ = _split_start([half, lax.empty(half.shape, half.dtype)], _plan_to_sibling, 1, "rs_sib_" + tag)
        sibs[tag] = state
        return t_

    def dw_mine(a, b, sh, tag, name, after):
        theirs = _split_wait(sibs.pop(tag), after, _plan_to_sibling, "rs_sibw_" + tag)[1]
        part = _mm_tn(a, b, name, sh=sh, which="mine", addend=theirs)
        land = lax.empty((3,) + sh.half_shape, part.dtype)
        state, t_ = _split_start([part, land], _plan_rs_ici(sh), 3, "rs_start_" + tag)
        icis[tag] = (state, sh)
        return t_

    def ffn_bwd(drb, a, gu, hb, wgu, wd, sfx):
        da = _mm_nt(drb, wd, BF16, "ffn_da")
        dgu = _swiglu_bwd(da, gu, "swiglu_bwd", after=dw_theirs(a, drb, sh_down, "d" + sfx, "ffn_dwd"))
        t_ = dw_mine(a, drb, sh_down, "d" + sfx, "ffn_dwd", dgu)
        t_ = dw_theirs(hb, dgu, sh_gu, "gu" + sfx, "ffn_dwgu", after=t_)
        return _mm_nt(dgu, wgu, F32, "ffn_dh", after=t_,
                      mid=lambda part: dw_mine(hb, dgu, sh_gu, "gu" + sfx, "ffn_dwgu", part))

    dh2 = ffn_bwd(dr3b, a2, gu2, h2b, wgu2, wd2, "2")

    dr2, dr2b, gs2, bs2 = _ln_bwd(dr3, dh2, xh2, rs2, ln2_g, 1.0, "ln_bwd")
    dymix = _mm_nt(dr2b, wout, F32, "mix_dy")
    db_, dc_, dx_, dconvw = _conv_bwd(u, dymix, conv_full, L, CC, "conv_bwd",
                                      after=dw_theirs(ymix, dr2b, sh_out, "out", "mix_dwout"))
    dz_, dpoolw, dpscale = _pool_bwd(dpool, dymix, poolw_full, pscale, L, CC, PG, "pool_bwd")
    t_sp = sib_begin(dpoolw.reshape(4 * G * PGs, PG).astype(BF16), sh_pool, "pool")
    du = jnp.concatenate([db_, dc_, dx_, dz_], axis=1)
    t_ = dw_mine(ymix, dr2b, sh_out, "out", "mix_dwout", du)
    t_ = dw_theirs(h1b, du, sh_in, "in", "mix_dwin", after=t_ + t_sp)
    dh1 = _mm_nt(du, win, F32, "mix_dh", after=t_ + ici_begin("pool", du))

    dr1, dr1b, gs1, bs1 = _ln_bwd(dr2, dh1, xh1, rs1, ln1_g, 0.5, "ln_bwd",
                                  after=dw_mine(h1b, du, sh_in, "in", "mix_dwin", dh1))
    dh0f = ffn_bwd(dr1b, a1, gu1, h0b, wgu1, wd1, "1")
    grad_x, gmeta = _input_grad(dr1, dh0f, L, "input_grad")

    names = ["meta_tokens", "ffn1_w_gu", "ffn1_w_down", "ln1_g", "ln1_b", "w_in", "conv_w", "pool_w", "pool_scale",
             "w_out", "ln2_g", "ln2_b", "ffn2_w_gu", "ffn2_w_down", "ln3_g", "ln3_b"]
    ws = dict(zip(names, [meta_tokens, ffn1_w_gu, ffn1_w_down, ln1_g, ln1_b, w_in, conv_w, pool_w, pool_scale, w_out,
                          ln2_g, ln2_b, ffn2_w_gu, ffn2_w_down, ln3_g, ln3_b]))
    ms = dict(zip(names, [m_meta_tokens, m_ffn1_w_gu, m_ffn1_w_down, m_ln1_g, m_ln1_b, m_w_in, m_conv_w, m_pool_w,
                          m_pool_scale, m_w_out, m_ln2_g, m_ln2_b, m_ffn2_w_gu, m_ffn2_w_down, m_ln3_g, m_ln3_b]))
    vs = dict(zip(names, [v_meta_tokens, v_ffn1_w_gu, v_ffn1_w_down, v_ln1_g, v_ln1_b, v_w_in, v_conv_w, v_pool_w,
                          v_pool_scale, v_w_out, v_ln2_g, v_ln2_b, v_ffn2_w_gu, v_ffn2_w_down, v_ln3_g, v_ln3_b]))
    grads, delta, new_m, new_v = {}, {}, {}, {}

    weight_of = {"d2": "ffn2_w_down", "gu2": "ffn2_w_gu", "out": "w_out", "pool": "pool_w", "in": "w_in",
                 "d1": "ffn1_w_down", "gu1": "ffn1_w_gu"}
    big = tuple(n for n in weight_of.values() if n != "pool_w")

    def finish(tag, after):
        n = weight_of[tag]
        g = share_end(tag, after)
        if n == "pool_w":
            grads[n] = g.reshape(1, G, PGs, PG)
            return g
        d_, m_, v_, g_ = _adamw(ws[n][0], g, ms[n][0], vs[n][0], "adamw_" + tag, copy_grad=True)
        grads[n], delta[n], new_m[n], new_v[n] = g_[None], d_[None], m_[None], v_[None]
        return d_

    last, prev = None, grad_x
    for tag in weight_of:
        t_ = share_begin(tag, prev)
        prev = t_ if last is None else finish(last, t_)
        last = tag

    small_g = [gs1, bs1, gs2, bs2, gs3, bs3, dpscale, dconvw, jnp.sum(gmeta, axis=0)]
    gpacked, goffs = _pack_rows(small_g, D)
    gsum = _exchange_small(gpacked, True, "ar_small", after=prev)
    (grads["ln1_g"], grads["ln1_b"], grads["ln2_g"], grads["ln2_b"], grads["ln3_g"], grads["ln3_b"],
     grads["pool_scale"], g_conv_all, g_meta_all) = _unpack_rows(gsum, goffs, [p.shape for p in small_g])
    grads["meta_tokens"] = lax.dynamic_slice_in_dim(g_meta_all, q * (D // 4), D // 4, axis=1)
    grads["conv_w"] = lax.dynamic_slice_in_dim(g_conv_all, q * (CC // 4), CC // 4, axis=1)[None]
    finish(last, gsum)

    small = [n for n in names if n not in big]
    width = 4 * LANES
    pw, poffs = _pack_rows([ws[n] for n in small], width)
    pg, _ = _pack_rows([grads[n] for n in small], width)
    pm, _ = _pack_rows([ms[n] for n in small], width)
    pv, _ = _pack_rows([vs[n] for n in small], width)
    sd, sm, sv = _adamw(pw, pg, pm, pv, "adamw_small")
    shapes = [ws[n].shape for n in small]
    for n, d_, m_, v_ in zip(small, _unpack_rows(sd, poffs, shapes), _unpack_rows(sm, poffs, shapes),
                             _unpack_rows(sv, poffs, shapes)):
        delta[n], new_m[n], new_v[n] = d_, m_, v_

    return (loss, grad_x, *[grads[n] for n in names], *[delta[n] for n in names], *[new_m[n] for n in names],
            *[new_v[n] for n in names])
```

```python
import jax
import jax.numpy as jnp
from jax import lax
from jax.experimental import pallas as pl
from jax.experimental.pallas import tpu as pltpu

F32 = jnp.float32
BF16 = jnp.bfloat16
MESH = pl.DeviceIdType.MESH
ALL_AXES = ("x", "y", "c")

N_META = 16
POOL_WINDOWS = (2, 4, 8, 16)
LN_EPS = 1e-5
ALPHA = 2.0 ** 0.25
ADAM_LR, ADAM_B1, ADAM_B2, ADAM_EPS, ADAM_WD, ADAM_STEP = 0.001, 0.9, 0.999, 1e-08, 0.01, 10

VMEM_LIMIT_V7X = 60 * 1024 * 1024
LANES = 128
ELEMENTWISE_BLOCK_BYTES = 2 * 1024 * 1024


def _params(**kw):
    return pltpu.CompilerParams(vmem_limit_bytes=VMEM_LIMIT_V7X, **kw)


def _tile(n, prefs):
    for t in prefs:
        if t <= n and n % t == 0:
            return t
    return n


def _rows_tile(rows, cols, mult=16):
    cap = max(mult, ELEMENTWISE_BLOCK_BYTES // (4 * cols))
    best = None
    for t in range(mult, min(rows, cap) + 1, mult):
        if rows % t == 0:
            best = t
    return best if best is not None else rows


def _matmul(a, b, *, mode, tm, tn, tk, out_dtype, name, a_outer=True, scale=1.0, after=None, k_off=0, nk=None,
            addend=None, m_range=None, n_range=None):
    if mode == "tn":
        K, M = a.shape
    else:
        M, K = a.shape
    N = b.shape[0] if mode == "nt" else b.shape[1]
    assert M % tm == 0 and N % tn == 0 and K % tk == 0, (name, M, N, K, tm, tn, tk)
    nk = K // tk if nk is None else nk
    m0, mt = m_range if m_range is not None else (lambda: 0, M // tm)
    n0, nt = n_range if n_range is not None else (lambda: 0, N // tn)
    if a_outer:
        grid = (mt, nt, nk)
        ij = lambda p, q: (p, q)
    else:
        grid = (nt, mt, nk)
        ij = lambda p, q: (q, p)

    if mode == "tn":
        a_spec = pl.BlockSpec((tk, tm), lambda p, q, k: (k + k_off, ij(p, q)[0] + m0()))
        dims = (((0,), (0,)), ((), ()))
    else:
        a_spec = pl.BlockSpec((tm, tk), lambda p, q, k: (ij(p, q)[0] + m0(), k + k_off))
        dims = (((1,), (1,)), ((), ())) if mode == "nt" else (((1,), (0,)), ((), ()))
    if mode == "nt":
        b_spec = pl.BlockSpec((tn, tk), lambda p, q, k: (ij(p, q)[1] + n0(), k + k_off))
    else:
        b_spec = pl.BlockSpec((tk, tn), lambda p, q, k: (k + k_off, ij(p, q)[1] + n0()))
    o_spec = pl.BlockSpec((tm, tn), lambda p, q, k: ij(p, q))

    extra, extra_specs = [], []
    if addend is not None:
        extra.append(addend)
        extra_specs.append(o_spec)
    if after is not None:
        extra.append(after)
        extra_specs.append(pl.BlockSpec(memory_space=pl.ANY))
    n_in = 2 + len(extra)

    def fin(acc, refs):
        if scale != 1.0:
            acc = acc * scale
        if addend is not None:
            acc = acc + refs[2][...].astype(F32)
        return acc.astype(out_dtype)

    if nk == 1:
        def body(*refs):
            a_ref, b_ref, o_ref = refs[0], refs[1], refs[n_in]
            o_ref[...] = fin(lax.dot_general(a_ref[...], b_ref[...], dims, preferred_element_type=F32), refs)
        scratch = []
    else:
        def body(*refs):
            a_ref, b_ref, o_ref, acc_ref = refs[0], refs[1], refs[n_in], refs[n_in + 1]
            k = pl.program_id(2)
            prod = lax.dot_general(a_ref[...], b_ref[...], dims, preferred_element_type=F32)

            @pl.when(k == 0)
            def _():
                acc_ref[...] = prod

            @pl.when(k > 0)
            def _():
                acc_ref[...] += prod

            @pl.when(k == nk - 1)
            def _():
                o_ref[...] = fin(acc_ref[...], refs)
        scratch = [pltpu.VMEM((tm, tn), F32)]

    return pl.pallas_call(
        body, name=name, grid=grid, in_specs=[a_spec, b_spec] + extra_specs, out_specs=o_spec,
        out_shape=jax.ShapeDtypeStruct((mt * tm, nt * tn), out_dtype), scratch_shapes=scratch,
        compiler_params=_params(),
    )(a, b, *extra)


TM_PREFS = (1376, 688, 512, 256, 128, 64)


def _mm_nn(a, b, out_dtype, name, after=None, k_half=None, addend=None):
    M, K = a.shape
    N = b.shape[1]
    big_k = K * 2 * 1376 > 12 * 1024 * 1024
    tm = _tile(M, (688,) + TM_PREFS[2:]) if big_k else _tile(M, TM_PREFS)
    tn = _tile(N, (256, 128)) if big_k else _tile(N, (512, 256, 128))
    if k_half is None:
        return _matmul(a, b, mode="nn", tm=tm, tn=tn, tk=K, out_dtype=out_dtype, name=name, after=after)
    return _matmul(a, b, mode="nn", tm=tm, tn=tn, tk=K // 2, k_off=k_half, nk=1, out_dtype=out_dtype, name=name,
                   after=after, addend=addend)


def _mm_nt(a, b, out_dtype, name, scale=1.0, after=None, mid=None):
    M, K = a.shape
    N = b.shape[0]
    mib = 1024 * 1024
    parts = 1
    while K // parts * 2 * 688 > 16 * mib and K % (2 * parts * LANES) == 0:
        parts *= 2
    kp = K // parts
    tm = _tile(M, TM_PREFS) if kp * 2 * 1376 <= 12 * mib else _tile(M, TM_PREFS[1:])
    tn = _tile(N, (512, 256, 128))
    if 4 * kp * (tm + tn) > 44 * mib:
        tn = _tile(N, (256, 128))
    acc = None
    for p in range(parts):
        last = p == parts - 1
        acc = _matmul(a, b, mode="nt", tm=tm, tn=tn, tk=kp, k_off=p, nk=1, out_dtype=out_dtype if last else F32,
                      name=name, scale=scale, after=after, addend=acc)
        after = mid(acc) if mid is not None and p == 0 and not last else None
    if mid is not None and parts == 1:
        mid(acc)
    return acc


def _mm_tn(a, b, name, after=None, sh=None, which=None, addend=None):
    T, M = a.shape
    N = b.shape[1]
    if M % 2048 == 0:
        tm, tn, a_outer = 2048, _tile(N, (512, 256, 128)), True
    else:
        tm, tn, a_outer = _tile(M, (256, 128)), _tile(N, (2048, 1024, 512, 256, 128)), False
    m_range = n_range = None
    if which is not None:
        owner = _core if which == "mine" else (lambda: 1 - _core())
        if sh.col_halves:
            tn = min(tn, sh.hc)
            n_range = (lambda: owner() * (sh.hc // tn), sh.hc // tn)
        else:
            tm = min(tm, sh.h)
            m_range = (lambda: owner() * (sh.h // tm), sh.h // tm)
    return _matmul(a, b, mode="tn", tm=tm, tn=tn, tk=T, out_dtype=BF16, name=name, a_outer=a_outer, after=after,
                   m_range=m_range, n_range=n_range, addend=addend)


def _call(body, args, after, *, in_specs, **kw):
    if after is None:
        return pl.pallas_call(body, in_specs=in_specs, **kw)(*args)
    n = len(args)

    def ordered(*refs):
        return body(*refs[:n], *refs[n + 1:])

    return pl.pallas_call(ordered, in_specs=list(in_specs) + [pl.BlockSpec(memory_space=pl.ANY)], **kw)(*args, after)


def _silu_mul(gu, name, after=None):
    T, F2 = gu.shape
    F = F2 // 2
    tm = _rows_tile(T, F)

    def body(g_ref, u_ref, o_ref):
        g = g_ref[...].astype(F32)
        o_ref[...] = (g * jax.nn.sigmoid(g) * u_ref[...].astype(F32)).astype(BF16)

    return _call(
        body, (gu, gu), after, name=name, grid=(T // tm,),
        in_specs=[pl.BlockSpec((tm, F), lambda i: (i, 0)), pl.BlockSpec((tm, F), lambda i: (i, 1))],
        out_specs=pl.BlockSpec((tm, F), lambda i: (i, 0)),
        out_shape=jax.ShapeDtypeStruct((T, F), BF16), compiler_params=_params(),
    )


def _swiglu_bwd(da, gu, name, after=None):
    T, F2 = gu.shape
    F = F2 // 2
    tm = _rows_tile(T, F)

    def body(da_ref, g_ref, u_ref, o_ref):
        g = g_ref[...].astype(F32)
        da_ = da_ref[...].astype(F32)
        s = jax.nn.sigmoid(g)
        o_ref[:, :F] = (da_ * u_ref[...].astype(F32) * (s * (1.0 + g * (1.0 - s)))).astype(BF16)
        o_ref[:, F:] = (da_ * (g * s)).astype(BF16)

    return _call(
        body, (da, gu, gu), after, name=name, grid=(T // tm,),
        in_specs=[pl.BlockSpec((tm, F), lambda i: (i, 0)), pl.BlockSpec((tm, F), lambda i: (i, 0)),
                  pl.BlockSpec((tm, F), lambda i: (i, 1))],
        out_specs=pl.BlockSpec((tm, F2), lambda i: (i, 0)),
        out_shape=jax.ShapeDtypeStruct((T, F2), BF16), compiler_params=_params(),
    )


def _ln_stats(r):
    mu = jnp.mean(r, axis=-1, keepdims=True)
    xc = r - mu
    var = jnp.mean(xc * xc, axis=-1, keepdims=True)
    rstd = lax.rsqrt(var + LN_EPS)
    return xc * rstd, rstd


def _ln_bwd_math(dh, xhat, rstd, g):
    dxh = dh * g
    m1 = jnp.mean(dxh, axis=-1, keepdims=True)
    m2 = jnp.mean(dxh * xhat, axis=-1, keepdims=True)
    return rstd * (dxh - m1 - xhat * m2)


def _ln_fwd(hprev, f, g, b, fscale, name, after=None):
    T, D = hprev.shape
    tm = _rows_tile(T, D)

    def body(hp_ref, f_ref, g_ref, b_ref, h_ref, hb_ref, xh_ref, rs_ref):
        r = ALPHA * hp_ref[...] + fscale * f_ref[...]
        xhat, rstd = _ln_stats(r)
        h = xhat * g_ref[...] + b_ref[...]
        h_ref[...] = h
        hb_ref[...] = h.astype(BF16)
        xh_ref[...] = xhat
        rs_ref[...] = rstd

    row = pl.BlockSpec((tm, D), lambda i: (i, 0))
    vec = pl.BlockSpec((1, D), lambda i: (0, 0))
    col = pl.BlockSpec((tm, 1), lambda i: (i, 0))
    return _call(
        body, (hprev, f, g, b), after, name=name, grid=(T // tm,), in_specs=[row, row, vec, vec],
        out_specs=[row, row, row, col],
        out_shape=[jax.ShapeDtypeStruct((T, D), F32), jax.ShapeDtypeStruct((T, D), BF16),
                   jax.ShapeDtypeStruct((T, D), F32), jax.ShapeDtypeStruct((T, 1), F32)],
        compiler_params=_params(),
    )


def _ln_bwd(dr_next, dh_branch, xhat, rstd, g, fscale, name, after=None):
    T, D = xhat.shape
    tm = _rows_tile(T, D)

    def body(dn_ref, db_ref, xh_ref, rs_ref, g_ref, dr_ref, drb_ref, gs_ref, bs_ref):
        i = pl.program_id(0)
        dh = ALPHA * dn_ref[...] + db_ref[...]
        xhat_ = xh_ref[...]
        dr = _ln_bwd_math(dh, xhat_, rs_ref[...], g_ref[...])
        dr_ref[...] = dr
        drb_ref[...] = (fscale * dr).astype(BF16)

        @pl.when(i == 0)
        def _():
            gs_ref[...] = jnp.zeros_like(gs_ref)
            bs_ref[...] = jnp.zeros_like(bs_ref)

        gs_ref[...] += jnp.sum(dh * xhat_, axis=0, keepdims=True)
        bs_ref[...] += jnp.sum(dh, axis=0, keepdims=True)

    row = pl.BlockSpec((tm, D), lambda i: (i, 0))
    vec = pl.BlockSpec((1, D), lambda i: (0, 0))
    col = pl.BlockSpec((tm, 1), lambda i: (i, 0))
    return _call(
        body, (dr_next, dh_branch, xhat, rstd, g), after, name=name, grid=(T // tm,),
        in_specs=[row, row, row, col, vec], out_specs=[row, row, vec, vec],
        out_shape=[jax.ShapeDtypeStruct((T, D), F32), jax.ShapeDtypeStruct((T, D), BF16),
                   jax.ShapeDtypeStruct((1, D), F32), jax.ShapeDtypeStruct((1, D), F32)],
        compiler_params=_params(),
    )


def _ln3_loss(hprev, f, target, mask, g, b, name):
    T, D = hprev.shape
    tm = _rows_tile(T, D)

    def body(hp_ref, f_ref, t_ref, m_ref, g_ref, b_ref, dr_ref, drb_ref, gs_ref, bs_ref, ls_ref):
        i = pl.program_id(0)
        r = ALPHA * hp_ref[...] + 0.5 * f_ref[...]
        xhat, rstd = _ln_stats(r)
        g_ = g_ref[...]
        y = xhat * g_ + b_ref[...]
        err = (y - t_ref[...]) * m_ref[...]
        dy = err * (1.0 / D)
        dr = _ln_bwd_math(dy, xhat, rstd, g_)
        dr_ref[...] = dr
        drb_ref[...] = (0.5 * dr).astype(BF16)

        @pl.when(i == 0)
        def _():
            gs_ref[...] = jnp.zeros_like(gs_ref)
            bs_ref[...] = jnp.zeros_like(bs_ref)
            ls_ref[...] = jnp.zeros_like(ls_ref)

        gs_ref[...] += jnp.sum(dy * xhat, axis=0, keepdims=True)
        bs_ref[...] += jnp.sum(dy, axis=0, keepdims=True)
        ls_ref[...] += jnp.sum(err * err, axis=0, keepdims=True)

    row = pl.BlockSpec((tm, D), lambda i: (i, 0))
    vec = pl.BlockSpec((1, D), lambda i: (0, 0))
    col = pl.BlockSpec((tm, 1), lambda i: (i, 0))
    return pl.pallas_call(
        body, name=name, grid=(T // tm,), in_specs=[row, row, row, col, vec, vec],
        out_specs=[row, row, vec, vec, vec],
        out_shape=[jax.ShapeDtypeStruct((T, D), F32), jax.ShapeDtypeStruct((T, D), BF16),
                   jax.ShapeDtypeStruct((1, D), F32), jax.ShapeDtypeStruct((1, D), F32),
                   jax.ShapeDtypeStruct((1, D), F32)],
        compiler_params=_params(),
    )(hprev, f, target, mask, g, b)


def _down(x, k):
    rows = lax.broadcasted_iota(jnp.int32, x.shape, 0)
    return jnp.where(rows >= k, pltpu.roll(x, k, axis=0), 0.0)


def _up(x, k):
    n = x.shape[0]
    rows = lax.broadcasted_iota(jnp.int32, x.shape, 0)
    return jnp.where(rows < n - k, pltpu.roll(x, n - k, axis=0), 0.0)


def _conv_fwd(u, conv_w, L, CC, name, after=None):
    T = u.shape[0]
    tc = _tile(CC, (256, 128))
    n = CC // tc

    def body(b_ref, c_ref, x_ref, w_ref, o_ref):
        cx = c_ref[...] * x_ref[...]
        w = w_ref[...]
        conv = w[0:1] * _down(cx, 2) + w[1:2] * _down(cx, 1) + w[2:3] * cx
        o_ref[...] = (b_ref[...] * conv).astype(BF16)

    blk = lambda off: pl.BlockSpec((L, tc), lambda s, j: (s, j + off * n))
    return _call(
        body, (u, u, u, conv_w), after, name=name, grid=(T // L, n),
        in_specs=[blk(0), blk(1), blk(2), pl.BlockSpec((3, tc), lambda s, j: (0, j))],
        out_specs=pl.BlockSpec((L, tc), lambda s, j: (s, j)),
        out_shape=jax.ShapeDtypeStruct((T, CC), BF16), compiler_params=_params(),
    )


def _conv_bwd(u, dy, conv_w, L, CC, name, after=None):
    T = u.shape[0]
    tc = _tile(CC, (256, 128))
    n = CC // tc

    def body(b_ref, c_ref, x_ref, dy_ref, w_ref, db_ref, dc_ref, dx_ref, dw_ref):
        s = pl.program_id(1)
        c_, x_ = c_ref[...], x_ref[...]
        cx = c_ * x_
        w = w_ref[...]
        cx1, cx2 = _down(cx, 1), _down(cx, 2)
        conv = w[0:1] * cx2 + w[1:2] * cx1 + w[2:3] * cx
        dy_ = dy_ref[...]
        db_ref[...] = (dy_ * conv).astype(BF16)
        dconv = dy_ * b_ref[...]
        dcx = w[2:3] * dconv + w[1:2] * _up(dconv, 1) + w[0:1] * _up(dconv, 2)
        dc_ref[...] = (dcx * x_).astype(BF16)
        dx_ref[...] = (dcx * c_).astype(BF16)

        @pl.when(s == 0)
        def _():
            dw_ref[...] = jnp.zeros_like(dw_ref)

        dw_ref[0:1, :] += jnp.sum(dconv * cx2, axis=0, keepdims=True)
        dw_ref[1:2, :] += jnp.sum(dconv * cx1, axis=0, keepdims=True)
        dw_ref[2:3, :] += jnp.sum(dconv * cx, axis=0, keepdims=True)

    blk = lambda off: pl.BlockSpec((L, tc), lambda j, s: (s, j + off * n))
    out = pl.BlockSpec((L, tc), lambda j, s: (s, j))
    act = jax.ShapeDtypeStruct((T, CC), BF16)
    return _call(
        body, (u, u, u, dy, conv_w), after, name=name, grid=(n, T // L),
        in_specs=[blk(0), blk(1), blk(2), blk(0), pl.BlockSpec((3, tc), lambda j, s: (0, j))],
        out_specs=[out, out, out, pl.BlockSpec((3, tc), lambda j, s: (0, j))],
        out_shape=[act, act, act, jax.ShapeDtypeStruct((3, CC), F32)], compiler_params=_params(),
    )


def _pool_counts(shape, g):
    rows = lax.broadcasted_iota(jnp.int32, shape, 0)
    win = jnp.left_shift(jnp.int32(POOL_WINDOWS[0]), g)
    return jnp.minimum(rows + 1, win).astype(F32)


def _pick_group(g, vals):
    out = vals[-1]
    for i in range(len(vals) - 2, -1, -1):
        out = jnp.where(g == i, vals[i], out)
    return out


def _pool_fwd(u, pool_w, pool_scale, L, CC, PG, name):
    T = u.shape[0]
    G = len(POOL_WINDOWS)
    off = 3 * CC // PG

    def body(z_ref, w_ref, sc_ref, y_ref, d_ref):
        g = pl.program_id(1)
        z = z_ref[...]
        s1 = z + _down(z, 1)
        s2 = s1 + _down(s1, 2)
        s3 = s2 + _down(s2, 4)
        s4 = s3 + _down(s3, 8)
        pooled = _pick_group(g, [s1, s2, s3, s4]) / _pool_counts(z.shape, g)
        d = (pooled - z).astype(BF16)
        d_ref[...] = d
        q = jnp.dot(d, w_ref[...], preferred_element_type=F32)
        y_ref[...] = (q * sc_ref[...]).astype(BF16)

    blk = pl.BlockSpec((L, PG), lambda s, g: (s, g))
    act = jax.ShapeDtypeStruct((T, G * PG), BF16)
    return pl.pallas_call(
        body, name=name, grid=(T // L, G),
        in_specs=[pl.BlockSpec((L, PG), lambda s, g: (s, off + g)), pl.BlockSpec((None, PG, PG), lambda s, g: (g, 0, 0)),
                  pl.BlockSpec((1, PG), lambda s, g: (0, g))],
        out_specs=[blk, blk], out_shape=[act, act], compiler_params=_params(),
    )(u, pool_w, pool_scale)


def _pool_bwd(d, dy, pool_w, pool_scale, L, CC, PG, name):
    T = d.shape[0]
    G = len(POOL_WINDOWS)
    off = CC // PG
    PGs = PG // 4

    def body(d_ref, dy_ref, w_ref, sc_ref, dz_ref, dw_ref, dsc_ref):
        g = pl.program_id(0)
        s = pl.program_id(1)
        d_ = d_ref[...]
        w = w_ref[...]
        dy_ = dy_ref[...]
        q = jnp.dot(d_, w, preferred_element_type=F32)
        dq = (dy_ * sc_ref[...]).astype(BF16)
        dd = lax.dot_general(dq, w, (((1,), (1,)), ((), ())), preferred_element_type=F32)
        e = dd / _pool_counts(dd.shape, g)
        a1 = e + _up(e, 1)
        a2 = a1 + _up(a1, 2)
        a3 = a2 + _up(a2, 4)
        a4 = a3 + _up(a3, 8)
        dz_ref[...] = (_pick_group(g, [a1, a2, a3, a4]) - dd).astype(BF16)

        @pl.when(s == 0)
        def _():
            dw_ref[...] = jnp.zeros_like(dw_ref)
            dsc_ref[...] = jnp.zeros_like(dsc_ref)

        dw = lax.dot_general(d_, dq, (((0,), (0,)), ((), ())), preferred_element_type=F32)
        for k in range(4):
            dw_ref[k] += dw[k * PGs:(k + 1) * PGs, :]
        dsc_ref[...] += jnp.sum(dy_ * q, axis=0, keepdims=True)

    return pl.pallas_call(
        body, name=name, grid=(G, T // L),
        in_specs=[pl.BlockSpec((L, PG), lambda g, s: (s, g)), pl.BlockSpec((L, PG), lambda g, s: (s, off + g)),
                  pl.BlockSpec((None, PG, PG), lambda g, s: (g, 0, 0)), pl.BlockSpec((1, PG), lambda g, s: (0, g))],
        out_specs=[pl.BlockSpec((L, PG), lambda g, s: (s, g)),
                   pl.BlockSpec((4, None, PGs, PG), lambda g, s: (0, g, 0, 0)),
                   pl.BlockSpec((1, PG), lambda g, s: (0, g))],
        out_shape=[jax.ShapeDtypeStruct((T, G * PG), BF16), jax.ShapeDtypeStruct((4, G, PGs, PG), F32),
                   jax.ShapeDtypeStruct((1, G * PG), F32)],
        compiler_params=_params(),
    )(d, dy, pool_w, pool_scale)


def _input_grad(dr1, dh_branch, L, name, after=None):
    T, D = dr1.shape
    nseq = T // L
    tc = _tile(D, (512, 256, 128))

    def body(a_ref, b_ref, gx_ref, gm_ref):
        dh = ALPHA * a_ref[...] + b_ref[...]
        gm_ref[...] = dh[:N_META]
        gx_ref[...] = dh[N_META:]

    blk = pl.BlockSpec((L, tc), lambda s, j: (s, j))
    return _call(
        body, (dr1, dh_branch), after, name=name, grid=(nseq, D // tc), in_specs=[blk, blk],
        out_specs=[pl.BlockSpec((None, L - N_META, tc), lambda s, j: (s, 0, j)),
                   pl.BlockSpec((None, N_META, tc), lambda s, j: (s, 0, j))],
        out_shape=[jax.ShapeDtypeStruct((nseq, L - N_META, D), F32), jax.ShapeDtypeStruct((nseq, N_META, D), F32)],
        compiler_params=_params(),
    )


def _chip():
    return 2 * lax.axis_index("x") + lax.axis_index("y")


def _core():
    return lax.axis_index("c")


def _cast_into_full(w, sh, name):
    tr = _rows_tile(sh.srows, sh.scols)
    nb = sh.srows // tr

    def body(w_ref, o_ref):
        o_ref[...] = w_ref[...].astype(BF16)

    if sh.axis == 0:
        out = pl.BlockSpec((tr, sh.scols), lambda i: (_chip() * nb + i, 0))
    else:
        out = pl.BlockSpec((tr, sh.scols), lambda i: (i, _chip()))
    return pl.pallas_call(
        body, name=name, grid=(nb,), in_specs=[pl.BlockSpec((tr, sh.scols), lambda i: (i, 0))], out_specs=out,
        out_shape=jax.ShapeDtypeStruct((sh.rows, sh.cols), BF16), compiler_params=_params(),
    )(w)


def _add_halves(dw, theirs, sh, name):
    R, C = sh.half_all_shape
    tr = _rows_tile(sh.h, C)
    nh, ns = sh.h // tr, sh.srows // tr

    def body(a_ref, b_ref, o_ref):
        o_ref[...] = (a_ref[...].astype(F32) + b_ref[...].astype(F32)).astype(BF16)

    blk = pl.BlockSpec((tr, C), lambda i: (i, 0))
    mine = pl.BlockSpec((tr, C), lambda i: ((i // nh) * ns + _core() * nh + i % nh, 0))
    return pl.pallas_call(body, name=name, grid=(R // tr,), in_specs=[mine, blk], out_specs=blk,
                          out_shape=jax.ShapeDtypeStruct((R, C), BF16), compiler_params=_params())(dw, theirs)


def _sum_parts(part, recv, sh, name):
    h, C = sh.half_shape
    tr = _rows_tile(h, C)
    nh = h // tr

    def body(o_ref, r_ref, out_ref):
        acc = o_ref[...].astype(F32)
        for k in range(3):
            acc = acc + r_ref[k].astype(F32)
        out_ref[...] = acc

    if sh.axis == 0:
        own = pl.BlockSpec((tr, C), lambda i: (_chip() * nh + i, 0))
    else:
        own = pl.BlockSpec((tr, C), lambda i: (i, _chip()))
    if sh.col_halves:
        out = pl.BlockSpec((tr, C), lambda i: (i, _core()))
    else:
        out = pl.BlockSpec((tr, C), lambda i: (_core() * nh + i, 0))
    return pl.pallas_call(
        body, name=name, grid=(nh,), in_specs=[own, pl.BlockSpec((3, tr, C), lambda i: (0, i, 0))], out_specs=out,
        out_shape=jax.ShapeDtypeStruct((sh.srows, sh.scols), F32), compiler_params=_params(),
    )(part, recv)


def _adamw(w, g, m, v, name, copy_grad=False):
    R, C = w.shape
    tr = _rows_tile(R, C, mult=8)

    def body(w_ref, g_ref, m_ref, v_ref, d_ref, nm_ref, nv_ref, *g_out):
        g_ = g_ref[...]
        for o in g_out:
            o[...] = g_
        m_ = ADAM_B1 * m_ref[...] + (1.0 - ADAM_B1) * g_
        v_ = ADAM_B2 * v_ref[...] + (1.0 - ADAM_B2) * (g_ * g_)
        m_hat = m_ / (1.0 - ADAM_B1 ** ADAM_STEP)
        v_hat = v_ / (1.0 - ADAM_B2 ** ADAM_STEP)
        d_ref[...] = -ADAM_LR * (m_hat / (jnp.sqrt(v_hat) + ADAM_EPS) + ADAM_WD * w_ref[...])
        nm_ref[...] = m_
        nv_ref[...] = v_

    blk = pl.BlockSpec((tr, C), lambda i: (i, 0))
    shp = jax.ShapeDtypeStruct((R, C), F32)
    n_out = 4 if copy_grad else 3
    return pl.pallas_call(body, name=name, grid=(R // tr,), in_specs=[blk] * 4, out_specs=[blk] * n_out,
                          out_shape=[shp] * n_out, compiler_params=_params())(w, g, m, v)


def _pos():
    return lax.axis_index("x"), lax.axis_index("y"), lax.axis_index("c")


def _other_chips(x, y):
    return [(1 - x, y), (x, 1 - y), (1 - x, 1 - y)]


class _Sharded:
    def __init__(self, full_shape, axis, col_halves=False):
        self.rows, self.cols = full_shape
        self.axis = axis
        self.srows = self.rows // 4 if axis == 0 else self.rows
        self.scols = self.cols if axis == 0 else self.cols // 4
        assert self.srows % 2 == 0
        self.h = self.srows // 2
        self.col_halves = col_halves
        assert not (col_halves and axis == 1)
        self.hc = self.cols // 2
        if col_halves:
            self.half_all_shape, self.half_shape = (self.rows, self.hc), (self.srows, self.hc)
        else:
            self.half_all_shape = (4 * self.h, self.cols) if axis == 0 else (self.h, self.cols)
            self.half_shape = (self.h, self.scols)

    def shard(self, ref, q):
        if self.axis == 0:
            return ref.at[pl.ds(q * self.srows, self.srows), :]
        return ref.at[:, pl.ds(pl.multiple_of(q * self.scols, LANES), self.scols)]

    def half(self, ref, q, c):
        if self.axis == 0:
            return ref.at[pl.ds(q * self.srows + c * self.h, self.h), :]
        return ref.at[pl.ds(c * self.h, self.h), pl.ds(pl.multiple_of(q * self.scols, LANES), self.scols)]

    def half_in_stack(self, ref, q):
        if self.col_halves:
            return ref.at[pl.ds(q * self.srows, self.srows), :]
        if self.axis == 0:
            return ref.at[pl.ds(q * self.h, self.h), :]
        return ref.at[:, pl.ds(pl.multiple_of(q * self.scols, LANES), self.scols)]


class _RowBatch:
    def __init__(self, sh, b):
        assert sh.axis == 1 and sh.rows % 4 == 0
        self.sh, self.b, self.h = sh, b, sh.rows // 4

    def half(self, ref, q, c):
        cols = pl.ds(pl.multiple_of(q * self.sh.scols, LANES), self.sh.scols)
        return ref.at[pl.ds((2 * self.b + c) * self.h, self.h), cols]


ANY = pl.BlockSpec(memory_space=pl.ANY)
DMA_ROW_ALIGN = 16
CHUNK_COUNTS = (16, 8, 43, 4, 2)


def _row_chunks(rows):
    n = 1
    if rows % DMA_ROW_ALIGN == 0:
        n = _tile(rows // DMA_ROW_ALIGN, CHUNK_COUNTS)
        n = n if n in CHUNK_COUNTS else 1
    size = rows // n
    return [(k * size, size) for k in range(n)]


class _Pieces:
    def __init__(self, src, dst, send_sem, recv_sem, to):
        self.args = (src, dst, send_sem, recv_sem, to)

    def _copy(self, rows=None):
        src, dst, send_sem, recv_sem, to = self.args
        if rows is not None:
            src, dst = src.at[pl.ds(*rows), :], dst.at[pl.ds(*rows), :]
        return pltpu.make_async_remote_copy(src_ref=src, dst_ref=dst, send_sem=send_sem, recv_sem=recv_sem,
                                            device_id=to, device_id_type=MESH)

    def start(self):
        for rows in _row_chunks(self.args[0].shape[0]):
            self._copy(rows).start()

    def wait_send(self):
        self._copy().wait_send()

    def wait_recv(self):
        self._copy().wait_recv()

    def wait(self):
        self._copy().wait()


HBM = pl.BlockSpec(memory_space=pltpu.HBM)
SEM = pl.BlockSpec(memory_space=pltpu.SEMAPHORE)
SPLIT_COPY = pltpu.CompilerParams(has_side_effects=pltpu.SideEffectType.DATAFLOW_SIDE_EFFECTING)
TOKEN = jax.ShapeDtypeStruct((8, LANES), F32)


def _in_hbm(a):
    return pltpu.with_memory_space_constraint(a, pltpu.HBM)


def _split_start(arrays, plan, n, name):
    na = len(arrays)

    def body(*refs):
        send_sems, recv_sems, token = refs[na], refs[na + 1], refs[-1]
        x, y, c = _pos()
        for j, (src, dst, to, _) in enumerate(plan(refs[:na], x, y, c)):
            _Pieces(src, dst, send_sems.at[j], recv_sems.at[j], to).start()
        token[...] = jnp.zeros_like(token)

    outs = pl.pallas_call(
        body, name=name, in_specs=(HBM,) * na,
        out_shape=(pltpu.SemaphoreType.DMA((n,)), pltpu.SemaphoreType.DMA((n,)),
                   *[pltpu.HBM(a.shape, a.dtype) for a in arrays], TOKEN),
        out_specs=(SEM, SEM) + (HBM,) * na + (pl.BlockSpec(memory_space=pltpu.VMEM),),
        input_output_aliases={i: 2 + i for i in range(na)}, compiler_params=SPLIT_COPY,
    )(*[_in_hbm(a) for a in arrays])
    return (outs[0], outs[1], list(outs[2:2 + na])), outs[-1]


def _split_wait(state, after, plan, name):
    send_sems, recv_sems, arrays = state
    na = len(arrays)

    def body(*refs):
        send_sems, recv_sems = refs[na], refs[na + 1]
        x, y, c = _pos()
        for j, (src, _, to, landing) in enumerate(plan(refs[:na], x, y, c)):
            cp = pltpu.make_async_remote_copy(src_ref=src, dst_ref=landing, send_sem=send_sems.at[j],
                                              recv_sem=recv_sems.at[j], device_id=to, device_id_type=MESH)
            cp.wait_send()
            cp.wait_recv()

    outs = pl.pallas_call(
        body, name=name, in_specs=(HBM,) * na + (SEM, SEM, ANY),
        out_shape=tuple(pltpu.HBM(a.shape, a.dtype) for a in arrays), out_specs=(HBM,) * na,
        input_output_aliases={i: i for i in range(na)}, compiler_params=SPLIT_COPY,
    )(*arrays, send_sems, recv_sems, after)
    return list(outs)


def _plan_gather_ici(sh):
    def plan(refs, x, y, c):
        (full,) = refs
        mine = sh.half(full, 2 * x + y, c)
        return [(mine, mine, (cx, cy, c), sh.half(full, 2 * cx + cy, c)) for cx, cy in _other_chips(x, y)]
    return plan


def _plan_gather_d2d(sh):
    def plan(refs, x, y, c):
        (full,) = refs
        out = []
        for cx, cy in _other_chips(x, y):
            landed = sh.half(full, 2 * cx + cy, c)
            out.append((landed, landed, (x, y, 1 - c), sh.half(full, 2 * cx + cy, 1 - c)))
        return out
    return plan


def _plan_rs_sibling(sh):
    def plan(refs, x, y, c):
        dw, theirs = refs
        if sh.axis == 1:
            return [(dw.at[pl.ds((1 - c) * sh.h, sh.h), :], theirs, (x, y, 1 - c), theirs)]
        out = []
        for k in range(4):
            give = dw.at[pl.ds(k * sh.srows + (1 - c) * sh.h, sh.h), :]
            theirs_k = theirs.at[pl.ds(k * sh.h, sh.h), :]
            out.append((give, theirs_k, (x, y, 1 - c), theirs_k))
        return out
    return plan


def _plan_rs_ici(sh):
    def plan(refs, x, y, c):
        part, land = refs
        return [(sh.half_in_stack(part, 2 * cx + cy), land.at[j], (cx, cy, c), land.at[j])
                for j, (cx, cy) in enumerate(_other_chips(x, y))]
    return plan


def _plan_share(sh):
    def plan(refs, x, y, c):
        (g,) = refs
        if sh.col_halves:
            mine = g.at[:, pl.ds(pl.multiple_of(c * sh.hc, LANES), sh.hc)]
            theirs = g.at[:, pl.ds(pl.multiple_of((1 - c) * sh.hc, LANES), sh.hc)]
        else:
            mine, theirs = g.at[pl.ds(c * sh.h, sh.h), :], g.at[pl.ds((1 - c) * sh.h, sh.h), :]
        return [(mine, mine, (x, y, 1 - c), theirs)]
    return plan


def _plan_to_sibling(refs, x, y, c):
    mine, theirs = refs
    return [(mine, theirs, (x, y, 1 - c), theirs)]


def _exchange_small(v, reduce, name, after=None):
    R, C = v.shape

    def body(v_ref, o_ref, buf, send_sems, recv_sems):
        x, y, c = _pos()
        me = 4 * x + 2 * y + c
        slots = buf if reduce else o_ref
        slots[me] = v_ref[...]
        cps = []
        for k in range(1, 8):
            px = 1 - x if k & 4 else x
            py = 1 - y if k & 2 else y
            pc = 1 - c if k & 1 else c
            cps.append((_Pieces(v_ref, slots.at[me], send_sems.at[k - 1], recv_sems.at[k - 1], (px, py, pc)),
                        4 * px + 2 * py + pc))
        for cp, _ in cps:
            cp.start()
        for k, (cp, peer) in enumerate(cps):
            pltpu.make_async_remote_copy(
                src_ref=v_ref, dst_ref=slots.at[peer], send_sem=send_sems.at[k], recv_sem=recv_sems.at[k],
                device_id=(x, y, c), device_id_type=MESH).wait_recv()
        if reduce:
            acc = buf[0]
            for d in range(1, 8):
                acc = acc + buf[d]
            o_ref[...] = acc
        for cp, _ in cps:
            cp.wait_send()

    vm = pl.BlockSpec(memory_space=pltpu.VMEM)
    out_shape = jax.ShapeDtypeStruct((R, C) if reduce else (8, R, C), v.dtype)
    scratch = [pltpu.VMEM((8, R, C) if reduce else (8, LANES), v.dtype),
               pltpu.SemaphoreType.DMA((7,)), pltpu.SemaphoreType.DMA((7,))]
    return _call(body, (v,), after, name=name, in_specs=[vm], out_specs=vm, out_shape=out_shape,
                 scratch_shapes=scratch, compiler_params=_params())


def _pack_rows(parts, width):
    rows, offs, at = [], [], 0
    for p in parts:
        flat = p.reshape(-1)
        n = 8 * -(-flat.shape[0] // (8 * width))
        flat = jnp.pad(flat, (0, n * width - flat.shape[0]))
        rows.append(flat.reshape(n, width))
        offs.append(at)
        at += n
    return jnp.concatenate(rows, axis=0), offs


def _unpack_rows(packed, offs, shapes):
    out = []
    for off, shp in zip(offs, shapes):
        size = 1
        for s in shp:
            size *= s
        n = -(-size // packed.shape[1])
        out.append(packed[off:off + n].reshape(-1)[:size].reshape(shp))
    return out


def kernel(x, meta_tokens, ffn1_w_gu, ffn1_w_down, ln1_g, ln1_b, w_in, conv_w, pool_w, pool_scale, w_out, ln2_g, ln2_b, ffn2_w_gu, ffn2_w_down, ln3_g, ln3_b, loss_target, m_meta_tokens, m_ffn1_w_gu, m_ffn1_w_down, m_ln1_g, m_ln1_b, m_w_in, m_conv_w, m_pool_w, m_pool_scale, m_w_out, m_ln2_g, m_ln2_b, m_ffn2_w_gu, m_ffn2_w_down, m_ln3_g, m_ln3_b, v_meta_tokens, v_ffn1_w_gu, v_ffn1_w_down, v_ln1_g, v_ln1_b, v_w_in, v_conv_w, v_pool_w, v_pool_scale, v_w_out, v_ln2_g, v_ln2_b, v_ffn2_w_gu, v_ffn2_w_down, v_ln3_g, v_ln3_b):
    nseq, S, D = x.shape
    L = S + N_META
    T = nseq * L
    F = ffn1_w_down.shape[1] * 4
    CC = conv_w.shape[2] * 4
    G, PGs, PG = pool_w.shape[1:]
    PC = G * PG
    IN = w_in.shape[2] * 4
    qx, qy, qc = _pos()
    q = 2 * qx + qy

    sh_gu = _Sharded((D, 2 * F), 1)
    sh_down = _Sharded((F, D), 0, col_halves=True)
    sh_in = _Sharded((D, IN), 1)
    sh_out = _Sharded((CC + PC, D), 0, col_halves=True)
    sh_pool = _Sharded((4 * G * PGs, PG), 0)

    gathers = {}
    tok = jnp.zeros((), F32)
    gu1_batches = [_RowBatch(sh_gu, 0), _RowBatch(sh_gu, 1)]
    gu1_buf, gu1_sems = [_cast_into_full(ffn1_w_gu[0], sh_gu, "cast_gu1")], []
    for b, batch in enumerate(gu1_batches):
        (send_sems, recv_sems, gu1_buf), t_ = _split_start(gu1_buf, _plan_gather_ici(batch), 3, "ag_start_gu1%d" % b)
        gu1_sems.append((send_sems, recv_sems))
        tok = tok + t_[0, 0]

    def gu1_batch(b, after):
        batch = gu1_batches[b]
        buf = _split_wait((*gu1_sems[b], gu1_buf), after, _plan_gather_ici(batch), "ag_wait_gu1%d" % b)
        state, t_ = _split_start(buf, _plan_gather_d2d(batch), 3, "ag_pass_gu1%d" % b)
        gu1_buf[:] = _split_wait(state, t_, _plan_gather_d2d(batch), "ag_got_gu1%d" % b)
        return gu1_buf[0]

    for tag, w, sh in (("d1", ffn1_w_down, sh_down), ("in", w_in, sh_in),
                       ("out", w_out, sh_out), ("gu2", ffn2_w_gu, sh_gu), ("d2", ffn2_w_down, sh_down)):
        full = _cast_into_full(w[0], sh, "cast_" + tag)
        state, t_ = _split_start([full], _plan_gather_ici(sh), 3, "ag_start_" + tag)
        gathers[tag] = (state, sh)
        tok = tok + t_[0, 0]

    def landed(tag, after):
        state, sh = gathers.pop(tag)
        (full,) = _split_wait(state, after, _plan_gather_ici(sh), "ag_wait_" + tag)
        state, t_ = _split_start([full], _plan_gather_d2d(sh), 3, "ag_pass_" + tag)
        gathers[tag] = (state, sh)
        return t_

    def arrive(tag, after):
        state, sh = gathers.pop(tag)
        return _split_wait(state, after, _plan_gather_d2d(sh), "ag_got_" + tag)[0]

    small_w = [meta_tokens, conv_w[0], pool_w[0]]
    packed, offs = _pack_rows(small_w, 4 * LANES)
    slots = _exchange_small(packed, False, "ag_small")
    per_chip = [_unpack_rows(slots[2 * k], offs, [p.shape for p in small_w]) for k in range(4)]
    meta_full = jnp.concatenate([p[0] for p in per_chip], axis=1)
    conv_full = jnp.concatenate([p[1] for p in per_chip], axis=1)
    poolw_full = jnp.concatenate([p[2] for p in per_chip], axis=1).astype(BF16)
    pscale = pool_scale

    h0 = jnp.concatenate([jnp.broadcast_to(meta_full[None], (nseq, N_META, D)), x], axis=1).reshape(T, D)
    h0b = (h0 + tok).astype(BF16)
    tgt = jnp.pad(loss_target, ((0, 0), (N_META, 0), (0, 0))).reshape(T, D)
    mask = (lax.broadcasted_iota(jnp.int32, (nseq, L, 1), 1) >= N_META).astype(F32).reshape(T, 1)

    gu1_half = _mm_nn(h0b, gu1_batch(0, h0b), F32, "ffn_gu", k_half=0)
    wgu1 = gu1_batch(1, gu1_half)
    gu1 = _mm_nn(h0b, wgu1, BF16, "ffn_gu", k_half=1, addend=gu1_half)
    a1 = _silu_mul(gu1, "silu_mul", after=landed("d1", gu1))
    wd1 = arrive("d1", a1)
    f1 = _mm_nn(a1, wd1, F32, "ffn_down")
    h1, h1b, xh1, rs1 = _ln_fwd(h0, f1, ln1_g, ln1_b, 0.5, "ln_fwd", after=landed("in", f1))
    win = arrive("in", h1b)

    u = _mm_nn(h1b, win, F32, "mix_in")
    yc = _conv_fwd(u, conv_full, L, CC, "conv_fwd", after=landed("out", u))
    yp, dpool = _pool_fwd(u, poolw_full, pscale, L, CC, PG, "pool_fwd")
    ymix = jnp.concatenate([yc, yp], axis=1)
    wout = arrive("out", ymix)
    o = _mm_nn(ymix, wout, F32, "mix_out")
    h2, h2b, xh2, rs2 = _ln_fwd(h1, o, ln2_g, ln2_b, 1.0, "ln_fwd", after=landed("gu2", o))
    wgu2 = arrive("gu2", h2b)

    gu2 = _mm_nn(h2b, wgu2, BF16, "ffn_gu")
    a2 = _silu_mul(gu2, "silu_mul", after=landed("d2", gu2))
    wd2 = arrive("d2", a2)
    f2 = _mm_nn(a2, wd2, F32, "ffn_down")
    dr3, dr3b, gs3, bs3, lsum = _ln3_loss(h2, f2, tgt, mask, ln3_g, ln3_b, "ln3_loss")
    loss = lax.psum(0.5 * jnp.sum(lsum) / D, ALL_AXES)

    sibs, icis, shares = {}, {}, {}

    def sib_begin(dw, sh, tag):
        theirs = lax.empty(sh.half_all_shape, dw.dtype)
        state, t_ = _split_start([dw, theirs], _plan_rs_sibling(sh), 4 if sh.axis == 0 else 1, "rs_sib_" + tag)
        sibs[tag] = (state, sh)
        return t_

    def ici_begin(tag, after):
        state, sh = sibs.pop(tag)
        dw, theirs = _split_wait(state, after, _plan_rs_sibling(sh), "rs_sibw_" + tag)
        part = _add_halves(dw, theirs, sh, "rs_add_" + tag)
        land = lax.empty((3,) + sh.half_shape, part.dtype)
        state, t_ = _split_start([part, land], _plan_rs_ici(sh), 3, "rs_start_" + tag)
        icis[tag] = (state, sh)
        return t_

    def share_begin(tag, after):
        state, sh = icis.pop(tag)
        part, recv = _split_wait(state, after, _plan_rs_ici(sh), "rs_wait_" + tag)
        g = _sum_parts(part, recv, sh, "rs_sum_" + tag)
        state, t_ = _split_start([g], _plan_share(sh), 1, "rs_share_" + tag)
        shares[tag] = (state, sh)
        return t_

    def share_end(tag, after):
        state, sh = shares.pop(tag)
        return _split_wait(state, after, _plan_share(sh), "rs_got_" + tag)[0]

    def dw_theirs(a, b, sh, tag, name, after=None):
        half = _mm_tn(a, b, name, after=after, sh=sh, which="theirs")
        state, t_ = _split_start([half, lax.empty(half.shape, half.dtype)], _plan_to_sibling, 1, "rs_sib_" + tag)
        sibs[tag] = state
        return t_

    def dw_mine(a, b, sh, tag, name, after):
        theirs = _split_wait(sibs.pop(tag), after, _plan_to_sibling, "rs_sibw_" + tag)[1]
        part = _mm_tn(a, b, name, sh=sh, which="mine", addend=theirs)
        land = lax.empty((3,) + sh.half_shape, part.dtype)
        state, t_ = _split_start([part, land], _plan_rs_ici(sh), 3, "rs_start_" + tag)
        icis[tag] = (state, sh)
        return t_

    def ffn_bwd(drb, a, gu, hb, wgu, wd, sfx):
        da = _mm_nt(drb, wd, BF16, "ffn_da")
        dgu = _swiglu_bwd(da, gu, "swiglu_bwd", after=dw_theirs(a, drb, sh_down, "d" + sfx, "ffn_dwd"))
        t_ = dw_mine(a, drb, sh_down, "d" + sfx, "ffn_dwd", dgu)
        t_ = dw_theirs(hb, dgu, sh_gu, "gu" + sfx, "ffn_dwgu", after=t_)
        return _mm_nt(dgu, wgu, F32, "ffn_dh", after=t_,
                      mid=lambda part: dw_mine(hb, dgu, sh_gu, "gu" + sfx, "ffn_dwgu", part))

    dh2 = ffn_bwd(dr3b, a2, gu2, h2b, wgu2, wd2, "2")

    dr2, dr2b, gs2, bs2 = _ln_bwd(dr3, dh2, xh2, rs2, ln2_g, 1.0, "ln_bwd")
    dymix = _mm_nt(dr2b, wout, F32, "mix_dy")
    db_, dc_, dx_, dconvw = _conv_bwd(u, dymix, conv_full, L, CC, "conv_bwd",
                                      after=dw_theirs(ymix, dr2b, sh_out, "out", "mix_dwout"))
    dz_, dpoolw, dpscale = _pool_bwd(dpool, dymix, poolw_full, pscale, L, CC, PG, "pool_bwd")
    t_sp = sib_begin(dpoolw.reshape(4 * G * PGs, PG).astype(BF16), sh_pool, "pool")
    du = jnp.concatenate([db_, dc_, dx_, dz_], axis=1)
    t_ = dw_mine(ymix, dr2b, sh_out, "out", "mix_dwout", du)
    t_ = dw_theirs(h1b, du, sh_in, "in", "mix_dwin", after=t_ + t_sp)
    dh1 = _mm_nt(du, win, F32, "mix_dh", after=t_ + ici_begin("pool", du))

    dr1, dr1b, gs1, bs1 = _ln_bwd(dr2, dh1, xh1, rs1, ln1_g, 0.5, "ln_bwd",
                                  after=dw_mine(h1b, du, sh_in, "in", "mix_dwin", dh1))
    dh0f = ffn_bwd(dr1b, a1, gu1, h0b, wgu1, wd1, "1")
    grad_x, gmeta = _input_grad(dr1, dh0f, L, "input_grad")

    names = ["meta_tokens", "ffn1_w_gu", "ffn1_w_down", "ln1_g", "ln1_b", "w_in", "conv_w", "pool_w", "pool_scale",
             "w_out", "ln2_g", "ln2_b", "ffn2_w_gu", "ffn2_w_down", "ln3_g", "ln3_b"]
    ws = dict(zip(names, [meta_tokens, ffn1_w_gu, ffn1_w_down, ln1_g, ln1_b, w_in, conv_w, pool_w, pool_scale, w_out,
                          ln2_g, ln2_b, ffn2_w_gu, ffn2_w_down, ln3_g, ln3_b]))
    ms = dict(zip(names, [m_meta_tokens, m_ffn1_w_gu, m_ffn1_w_down, m_ln1_g, m_ln1_b, m_w_in, m_conv_w, m_pool_w,
                          m_pool_scale, m_w_out, m_ln2_g, m_ln2_b, m_ffn2_w_gu, m_ffn2_w_down, m_ln3_g, m_ln3_b]))
    vs = dict(zip(names, [v_meta_tokens, v_ffn1_w_gu, v_ffn1_w_down, v_ln1_g, v_ln1_b, v_w_in, v_conv_w, v_pool_w,
                          v_pool_scale, v_w_out, v_ln2_g, v_ln2_b, v_ffn2_w_gu, v_ffn2_w_down, v_ln3_g, v_ln3_b]))
    grads, delta, new_m, new_v = {}, {}, {}, {}

    weight_of = {"d2": "ffn2_w_down", "gu2": "ffn2_w_gu", "out": "w_out", "pool": "pool_w", "in": "w_in",
                 "d1": "ffn1_w_down", "gu1": "ffn1_w_gu"}
    big = tuple(n for n in weight_of.values() if n != "pool_w")

    def finish(tag, after):
        n = weight_of[tag]
        g = share_end(tag, after)
        if n == "pool_w":
            grads[n] = g.reshape(1, G, PGs, PG)
            return g
        d_, m_, v_, g_ = _adamw(ws[n][0], g, ms[n][0], vs[n][0], "adamw_" + tag, copy_grad=True)
        grads[n], delta[n], new_m[n], new_v[n] = g_[None], d_[None], m_[None], v_[None]
        return d_

    last, prev = None, grad_x
    for tag in weight_of:
        t_ = share_begin(tag, prev)
        prev = t_ if last is None else finish(last, t_)
        last = tag

    small_g = [gs1, bs1, gs2, bs2, gs3, bs3, dpscale, dconvw, jnp.sum(gmeta, axis=0)]
    gpacked, goffs = _pack_rows(small_g, D)
    gsum = _exchange_small(gpacked, True, "ar_small", after=prev)
    (grads["ln1_g"], grads["ln1_b"], grads["ln2_g"], grads["ln2_b"], grads["ln3_g"], grads["ln3_b"],
     grads["pool_scale"], g_conv_all, g_meta_all) = _unpack_rows(gsum, goffs, [p.shape for p in small_g])
    grads["meta_tokens"] = lax.dynamic_slice_in_dim(g_meta_all, q * (D // 4), D // 4, axis=1)
    grads["conv_w"] = lax.dynamic_slice_in_dim(g_conv_all, q * (CC // 4), CC // 4, axis=1)[None]
    finish(last, gsum)

    small = [n for n in names if n not in big]
    width = 4 * LANES
    pw, poffs = _pack_rows([ws[n] for n in small], width)
    pg, _ = _pack_rows([grads[n] for n in small], width)
    pm, _ = _pack_rows([ms[n] for n in small], width)
    pv, _ = _pack_rows([vs[n] for n in small], width)
    sd, sm, sv = _adamw(pw, pg, pm, pv, "adamw_small")
    shapes = [ws[n].shape for n in small]
    for n, d_, m_, v_ in zip(small, _unpack_rows(sd, poffs, shapes), _unpack_rows(sm, poffs, shapes),
                             _unpack_rows(sv, poffs, shapes)):
        delta[n], new_m[n], new_v[n] = d_, m_, v_

    return (loss, grad_x, *[grads[n] for n in names], *[delta[n] for n in names], *[new_m[n] for n in names],
            *[new_v[n] for n in names])
```

```python
import jax
import jax.numpy as jnp
from jax import lax
from jax.experimental import pallas as pl
from jax.experimental.pallas import tpu as pltpu

F32 = jnp.float32
BF16 = jnp.bfloat16
MESH = pl.DeviceIdType.MESH
ALL_AXES = ("x", "y", "c")

N_META = 16
POOL_WINDOWS = (2, 4, 8, 16)
LN_EPS = 1e-5
ALPHA = 2.0 ** 0.25
ADAM_LR, ADAM_B1, ADAM_B2, ADAM_EPS, ADAM_WD, ADAM_STEP = 0.001, 0.9, 0.999, 1e-08, 0.01, 10

VMEM_LIMIT_V7X = 60 * 1024 * 1024
LANES = 128
ELEMENTWISE_BLOCK_BYTES = 2 * 1024 * 1024


def _params(**kw):
    return pltpu.CompilerParams(vmem_limit_bytes=VMEM_LIMIT_V7X, **kw)


def _tile(n, prefs):
    for t in prefs:
        if t <= n and n % t == 0:
            return t
    return n


def _rows_tile(rows, cols, mult=16):
    cap = max(mult, ELEMENTWISE_BLOCK_BYTES // (4 * cols))
    best = None
    for t in range(mult, min(rows, cap) + 1, mult):
        if rows % t == 0:
            best = t
    return best if best is not None else rows


def _matmul(a, b, *, mode, tm, tn, tk, out_dtype, name, a_outer=True, scale=1.0, after=None, k_off=0, nk=None,
            addend=None, m_range=None, n_range=None):
    if mode == "tn":
        K, M = a.shape
    else:
        M, K = a.shape
    N = b.shape[0] if mode == "nt" else b.shape[1]
    assert M % tm == 0 and N % tn == 0 and K % tk == 0, (name, M, N, K, tm, tn, tk)
    nk = K // tk if nk is None else nk
    m0, mt = m_range if m_range is not None else (lambda: 0, M // tm)
    n0, nt = n_range if n_range is not None else (lambda: 0, N // tn)
    if a_outer:
        grid = (mt, nt, nk)
        ij = lambda p, q: (p, q)
    else:
        grid = (nt, mt, nk)
        ij = lambda p, q: (q, p)

    if mode == "tn":
        a_spec = pl.BlockSpec((tk, tm), lambda p, q, k: (k + k_off, ij(p, q)[0] + m0()))
        dims = (((0,), (0,)), ((), ()))
    else:
        a_spec = pl.BlockSpec((tm, tk), lambda p, q, k: (ij(p, q)[0] + m0(), k + k_off))
        dims = (((1,), (1,)), ((), ())) if mode == "nt" else (((1,), (0,)), ((), ()))
    if mode == "nt":
        b_spec = pl.BlockSpec((tn, tk), lambda p, q, k: (ij(p, q)[1] + n0(), k + k_off))
    else:
        b_spec = pl.BlockSpec((tk, tn), lambda p, q, k: (k + k_off, ij(p, q)[1] + n0()))
    o_spec = pl.BlockSpec((tm, tn), lambda p, q, k: ij(p, q))

    extra, extra_specs = [], []
    if addend is not None:
        extra.append(addend)
        extra_specs.append(o_spec)
    if after is not None:
        extra.append(after)
        extra_specs.append(pl.BlockSpec(memory_space=pl.ANY))
    n_in = 2 + len(extra)

    def fin(acc, refs):
        if scale != 1.0:
            acc = acc * scale
        if addend is not None:
            acc = acc + refs[2][...].astype(F32)
        return acc.astype(out_dtype)

    if nk == 1:
        def body(*refs):
            a_ref, b_ref, o_ref = refs[0], refs[1], refs[n_in]
            o_ref[...] = fin(lax.dot_general(a_ref[...], b_ref[...], dims, preferred_element_type=F32), refs)
        scratch = []
    else:
        def body(*refs):
            a_ref, b_ref, o_ref, acc_ref = refs[0], refs[1], refs[n_in], refs[n_in + 1]
            k = pl.program_id(2)
            prod = lax.dot_general(a_ref[...], b_ref[...], dims, preferred_element_type=F32)

            @pl.when(k == 0)
            def _():
                acc_ref[...] = prod

            @pl.when(k > 0)
            def _():
                acc_ref[...] += prod

            @pl.when(k == nk - 1)
            def _():
                o_ref[...] = fin(acc_ref[...], refs)
        scratch = [pltpu.VMEM((tm, tn), F32)]

    return pl.pallas_call(
        body, name=name, grid=grid, in_specs=[a_spec, b_spec] + extra_specs, out_specs=o_spec,
        out_shape=jax.ShapeDtypeStruct((mt * tm, nt * tn), out_dtype), scratch_shapes=scratch,
        compiler_params=_params(),
    )(a, b, *extra)


TM_PREFS = (1376, 688, 512, 256, 128, 64)


def _mm_nn(a, b, out_dtype, name, after=None, k_half=None, addend=None):
    M, K = a.shape
    N = b.shape[1]
    panel = 16 * 1024 * 1024
    kp = K if k_half is None else K // 2
    big_k = kp * 2 * 1376 > panel
    if k_half is None and big_k and K % (2 * LANES) == 0 and K * 1376 <= panel:
        first = _mm_nn(a, b, F32, name, after=after, k_half=0)
        return _mm_nn(a, b, out_dtype, name, k_half=1, addend=first)
    tm = _tile(M, (688,) + TM_PREFS[2:]) if big_k else _tile(M, TM_PREFS)
    tn = _tile(N, (256, 128)) if big_k else _tile(N, (512, 256, 128))
    if k_half is None:
        return _matmul(a, b, mode="nn", tm=tm, tn=tn, tk=K, out_dtype=out_dtype, name=name, after=after)
    return _matmul(a, b, mode="nn", tm=tm, tn=tn, tk=K // 2, k_off=k_half, nk=1, out_dtype=out_dtype, name=name,
                   after=after, addend=addend)


def _mm_nt(a, b, out_dtype, name, scale=1.0, after=None, mid=None):
    M, K = a.shape
    N = b.shape[0]
    mib = 1024 * 1024
    parts = 1
    while K // parts * 2 * 1376 > 16 * mib and K % (2 * parts * LANES) == 0:
        parts *= 2
    kp = K // parts
    tm = _tile(M, TM_PREFS) if kp * 2 * 1376 <= 16 * mib else _tile(M, TM_PREFS[1:])
    tn = _tile(N, (512, 256, 128))
    if 4 * kp * (tm + tn) > 44 * mib:
        tn = _tile(N, (256, 128))
    acc = None
    for p in range(parts):
        last = p == parts - 1
        acc = _matmul(a, b, mode="nt", tm=tm, tn=tn, tk=kp, k_off=p, nk=1, out_dtype=out_dtype if last else F32,
                      name=name, scale=scale, after=after, addend=acc)
        after = mid(acc) if mid is not None and p == 0 and not last else None
    if mid is not None and parts == 1:
        mid(acc)
    return acc


def _mm_tn(a, b, name, after=None, sh=None, which=None, addend=None):
    T, M = a.shape
    N = b.shape[1]
    if M % 2048 == 0:
        tm, tn, a_outer = 2048, _tile(N, (512, 256, 128)), True
    else:
        tm, tn, a_outer = _tile(M, (256, 128)), _tile(N, (2048, 1024, 512, 256, 128)), False
    m_range = n_range = None
    if which is not None:
        owner = _core if which == "mine" else (lambda: 1 - _core())
        if sh.col_halves:
            tn = min(tn, sh.hc)
            n_range = (lambda: owner() * (sh.hc // tn), sh.hc // tn)
        else:
            tm = min(tm, sh.h)
            m_range = (lambda: owner() * (sh.h // tm), sh.h // tm)
    return _matmul(a, b, mode="tn", tm=tm, tn=tn, tk=T, out_dtype=BF16, name=name, a_outer=a_outer, after=after,
                   m_range=m_range, n_range=n_range, addend=addend)


def _call(body, args, after, *, in_specs, **kw):
    if after is None:
        return pl.pallas_call(body, in_specs=in_specs, **kw)(*args)
    n = len(args)

    def ordered(*refs):
        return body(*refs[:n], *refs[n + 1:])

    return pl.pallas_call(ordered, in_specs=list(in_specs) + [pl.BlockSpec(memory_space=pl.ANY)], **kw)(*args, after)


def _silu_mul(gu, name, after=None):
    T, F2 = gu.shape
    F = F2 // 2
    tm = _rows_tile(T, F // 2)

    def body(g_ref, u_ref, o_ref):
        g = g_ref[...].astype(F32)
        o_ref[...] = (g * jax.nn.sigmoid(g) * u_ref[...].astype(F32)).astype(BF16)

    return _call(
        body, (gu, gu), after, name=name, grid=(T // tm,),
        in_specs=[pl.BlockSpec((tm, F), lambda i: (i, 0)), pl.BlockSpec((tm, F), lambda i: (i, 1))],
        out_specs=pl.BlockSpec((tm, F), lambda i: (i, 0)),
        out_shape=jax.ShapeDtypeStruct((T, F), BF16), compiler_params=_params(),
    )


def _swiglu_bwd(da, gu, name, after=None):
    T, F2 = gu.shape
    F = F2 // 2
    tm = _rows_tile(T, F // 2)

    def body(da_ref, g_ref, u_ref, o_ref):
        g = g_ref[...].astype(F32)
        da_ = da_ref[...].astype(F32)
        s = jax.nn.sigmoid(g)
        o_ref[:, :F] = (da_ * u_ref[...].astype(F32) * (s * (1.0 + g * (1.0 - s)))).astype(BF16)
        o_ref[:, F:] = (da_ * (g * s)).astype(BF16)

    return _call(
        body, (da, gu, gu), after, name=name, grid=(T // tm,),
        in_specs=[pl.BlockSpec((tm, F), lambda i: (i, 0)), pl.BlockSpec((tm, F), lambda i: (i, 0)),
                  pl.BlockSpec((tm, F), lambda i: (i, 1))],
        out_specs=pl.BlockSpec((tm, F2), lambda i: (i, 0)),
        out_shape=jax.ShapeDtypeStruct((T, F2), BF16), compiler_params=_params(),
    )


def _ln_stats(r):
    mu = jnp.mean(r, axis=-1, keepdims=True)
    xc = r - mu
    var = jnp.mean(xc * xc, axis=-1, keepdims=True)
    rstd = lax.rsqrt(var + LN_EPS)
    return xc * rstd, rstd


def _ln_bwd_math(dh, xhat, rstd, g):
    dxh = dh * g
    m1 = jnp.mean(dxh, axis=-1, keepdims=True)
    m2 = jnp.mean(dxh * xhat, axis=-1, keepdims=True)
    return rstd * (dxh - m1 - xhat * m2)


def _ln_fwd(hprev, f, g, b, fscale, name, after=None):
    T, D = hprev.shape
    tm = _rows_tile(T, D)

    def body(hp_ref, f_ref, g_ref, b_ref, h_ref, hb_ref, xh_ref, rs_ref):
        r = ALPHA * hp_ref[...] + fscale * f_ref[...]
        xhat, rstd = _ln_stats(r)
        h = xhat * g_ref[...] + b_ref[...]
        h_ref[...] = h
        hb_ref[...] = h.astype(BF16)
        xh_ref[...] = xhat
        rs_ref[...] = rstd

    row = pl.BlockSpec((tm, D), lambda i: (i, 0))
    vec = pl.BlockSpec((1, D), lambda i: (0, 0))
    col = pl.BlockSpec((tm, 1), lambda i: (i, 0))
    return _call(
        body, (hprev, f, g, b), after, name=name, grid=(T // tm,), in_specs=[row, row, vec, vec],
        out_specs=[row, row, row, col],
        out_shape=[jax.ShapeDtypeStruct((T, D), F32), jax.ShapeDtypeStruct((T, D), BF16),
                   jax.ShapeDtypeStruct((T, D), F32), jax.ShapeDtypeStruct((T, 1), F32)],
        compiler_params=_params(),
    )


def _ln_bwd(dr_next, dh_branch, xhat, rstd, g, fscale, name, after=None):
    T, D = xhat.shape
    tm = _rows_tile(T, D)

    def body(dn_ref, db_ref, xh_ref, rs_ref, g_ref, dr_ref, drb_ref, gs_ref, bs_ref):
        i = pl.program_id(0)
        dh = ALPHA * dn_ref[...] + db_ref[...]
        xhat_ = xh_ref[...]
        dr = _ln_bwd_math(dh, xhat_, rs_ref[...], g_ref[...])
        dr_ref[...] = dr
        drb_ref[...] = (fscale * dr).astype(BF16)

        @pl.when(i == 0)
        def _():
            gs_ref[...] = jnp.zeros_like(gs_ref)
            bs_ref[...] = jnp.zeros_like(bs_ref)

        gs_ref[...] += jnp.sum(dh * xhat_, axis=0, keepdims=True)
        bs_ref[...] += jnp.sum(dh, axis=0, keepdims=True)

    row = pl.BlockSpec((tm, D), lambda i: (i, 0))
    vec = pl.BlockSpec((1, D), lambda i: (0, 0))
    col = pl.BlockSpec((tm, 1), lambda i: (i, 0))
    return _call(
        body, (dr_next, dh_branch, xhat, rstd, g), after, name=name, grid=(T // tm,),
        in_specs=[row, row, row, col, vec], out_specs=[row, row, vec, vec],
        out_shape=[jax.ShapeDtypeStruct((T, D), F32), jax.ShapeDtypeStruct((T, D), BF16),
                   jax.ShapeDtypeStruct((1, D), F32), jax.ShapeDtypeStruct((1, D), F32)],
        compiler_params=_params(),
    )


def _ln3_loss(hprev, f, target, mask, g, b, name):
    T, D = hprev.shape
    tm = _rows_tile(T, D)

    def body(hp_ref, f_ref, t_ref, m_ref, g_ref, b_ref, dr_ref, drb_ref, gs_ref, bs_ref, ls_ref):
        i = pl.program_id(0)
        r = ALPHA * hp_ref[...] + 0.5 * f_ref[...]
        xhat, rstd = _ln_stats(r)
        g_ = g_ref[...]
        y = xhat * g_ + b_ref[...]
        err = (y - t_ref[...]) * m_ref[...]
        dy = err * (1.0 / D)
        dr = _ln_bwd_math(dy, xhat, rstd, g_)
        dr_ref[...] = dr
        drb_ref[...] = (0.5 * dr).astype(BF16)

        @pl.when(i == 0)
        def _():
            gs_ref[...] = jnp.zeros_like(gs_ref)
            bs_ref[...] = jnp.zeros_like(bs_ref)
            ls_ref[...] = jnp.zeros_like(ls_ref)

        gs_ref[...] += jnp.sum(dy * xhat, axis=0, keepdims=True)
        bs_ref[...] += jnp.sum(dy, axis=0, keepdims=True)
        ls_ref[...] += jnp.sum(err * err, axis=0, keepdims=True)

    row = pl.BlockSpec((tm, D), lambda i: (i, 0))
    vec = pl.BlockSpec((1, D), lambda i: (0, 0))
    col = pl.BlockSpec((tm, 1), lambda i: (i, 0))
    return pl.pallas_call(
        body, name=name, grid=(T // tm,), in_specs=[row, row, row, col, vec, vec],
        out_specs=[row, row, vec, vec, vec],
        out_shape=[jax.ShapeDtypeStruct((T, D), F32), jax.ShapeDtypeStruct((T, D), BF16),
                   jax.ShapeDtypeStruct((1, D), F32), jax.ShapeDtypeStruct((1, D), F32),
                   jax.ShapeDtypeStruct((1, D), F32)],
        compiler_params=_params(),
    )(hprev, f, target, mask, g, b)


def _down(x, k):
    rows = lax.broadcasted_iota(jnp.int32, x.shape, 0)
    return jnp.where(rows >= k, pltpu.roll(x, k, axis=0), 0.0)


def _up(x, k):
    n = x.shape[0]
    rows = lax.broadcasted_iota(jnp.int32, x.shape, 0)
    return jnp.where(rows < n - k, pltpu.roll(x, n - k, axis=0), 0.0)


def _conv_fwd(u, conv_w, L, CC, name, after=None):
    T = u.shape[0]
    tc = _tile(CC, (256, 128))
    n = CC // tc

    def body(b_ref, c_ref, x_ref, w_ref, o_ref):
        cx = c_ref[...] * x_ref[...]
        w = w_ref[...]
        conv = w[0:1] * _down(cx, 2) + w[1:2] * _down(cx, 1) + w[2:3] * cx
        o_ref[...] = (b_ref[...] * conv).astype(BF16)

    blk = lambda off: pl.BlockSpec((L, tc), lambda s, j: (s, j + off * n))
    return _call(
        body, (u, u, u, conv_w), after, name=name, grid=(T // L, n),
        in_specs=[blk(0), blk(1), blk(2), pl.BlockSpec((3, tc), lambda s, j: (0, j))],
        out_specs=pl.BlockSpec((L, tc), lambda s, j: (s, j)),
        out_shape=jax.ShapeDtypeStruct((T, CC), BF16), compiler_params=_params(),
    )


def _conv_bwd(u, dy, conv_w, L, CC, name, after=None):
    T = u.shape[0]
    tc = _tile(CC, (256, 128))
    n = CC // tc

    def body(b_ref, c_ref, x_ref, dy_ref, w_ref, db_ref, dc_ref, dx_ref, dw_ref):
        s = pl.program_id(1)
        c_, x_ = c_ref[...], x_ref[...]
        cx = c_ * x_
        w = w_ref[...]
        cx1, cx2 = _down(cx, 1), _down(cx, 2)
        conv = w[0:1] * cx2 + w[1:2] * cx1 + w[2:3] * cx
        dy_ = dy_ref[...]
        db_ref[...] = (dy_ * conv).astype(BF16)
        dconv = dy_ * b_ref[...]
        dcx = w[2:3] * dconv + w[1:2] * _up(dconv, 1) + w[0:1] * _up(dconv, 2)
        dc_ref[...] = (dcx * x_).astype(BF16)
        dx_ref[...] = (dcx * c_).astype(BF16)

        @pl.when(s == 0)
        def _():
            dw_ref[...] = jnp.zeros_like(dw_ref)

        dw_ref[0:1, :] += jnp.sum(dconv * cx2, axis=0, keepdims=True)
        dw_ref[1:2, :] += jnp.sum(dconv * cx1, axis=0, keepdims=True)
        dw_ref[2:3, :] += jnp.sum(dconv * cx, axis=0, keepdims=True)

    blk = lambda off: pl.BlockSpec((L, tc), lambda j, s: (s, j + off * n))
    out = pl.BlockSpec((L, tc), lambda j, s: (s, j))
    act = jax.ShapeDtypeStruct((T, CC), BF16)
    return _call(
        body, (u, u, u, dy, conv_w), after, name=name, grid=(n, T // L),
        in_specs=[blk(0), blk(1), blk(2), blk(0), pl.BlockSpec((3, tc), lambda j, s: (0, j))],
        out_specs=[out, out, out, pl.BlockSpec((3, tc), lambda j, s: (0, j))],
        out_shape=[act, act, act, jax.ShapeDtypeStruct((3, CC), F32)], compiler_params=_params(),
    )


def _pool_counts(shape, g):
    rows = lax.broadcasted_iota(jnp.int32, shape, 0)
    win = jnp.left_shift(jnp.int32(POOL_WINDOWS[0]), g)
    return jnp.minimum(rows + 1, win).astype(F32)


def _pick_group(g, vals):
    out = vals[-1]
    for i in range(len(vals) - 2, -1, -1):
        out = jnp.where(g == i, vals[i], out)
    return out


def _pool_fwd(u, pool_w, pool_scale, L, CC, PG, name):
    T = u.shape[0]
    G = len(POOL_WINDOWS)
    off = 3 * CC // PG

    def body(z_ref, w_ref, sc_ref, y_ref, d_ref):
        g = pl.program_id(1)
        z = z_ref[...]
        s1 = z + _down(z, 1)
        s2 = s1 + _down(s1, 2)
        s3 = s2 + _down(s2, 4)
        s4 = s3 + _down(s3, 8)
        pooled = _pick_group(g, [s1, s2, s3, s4]) / _pool_counts(z.shape, g)
        d = (pooled - z).astype(BF16)
        d_ref[...] = d
        q = jnp.dot(d, w_ref[...], preferred_element_type=F32)
        y_ref[...] = (q * sc_ref[...]).astype(BF16)

    blk = pl.BlockSpec((L, PG), lambda s, g: (s, g))
    act = jax.ShapeDtypeStruct((T, G * PG), BF16)
    return pl.pallas_call(
        body, name=name, grid=(T // L, G),
        in_specs=[pl.BlockSpec((L, PG), lambda s, g: (s, off + g)), pl.BlockSpec((None, PG, PG), lambda s, g: (g, 0, 0)),
                  pl.BlockSpec((1, PG), lambda s, g: (0, g))],
        out_specs=[blk, blk], out_shape=[act, act], compiler_params=_params(),
    )(u, pool_w, pool_scale)


def _pool_bwd(d, dy, pool_w, pool_scale, L, CC, PG, name):
    T = d.shape[0]
    G = len(POOL_WINDOWS)
    off = CC // PG
    PGs = PG // 4

    def body(d_ref, dy_ref, w_ref, sc_ref, dz_ref, dw_ref, dsc_ref):
        g = pl.program_id(0)
        s = pl.program_id(1)
        d_ = d_ref[...]
        w = w_ref[...]
        dy_ = dy_ref[...]
        q = jnp.dot(d_, w, preferred_element_type=F32)
        dq = (dy_ * sc_ref[...]).astype(BF16)
        dd = lax.dot_general(dq, w, (((1,), (1,)), ((), ())), preferred_element_type=F32)
        e = dd / _pool_counts(dd.shape, g)
        a1 = e + _up(e, 1)
        a2 = a1 + _up(a1, 2)
        a3 = a2 + _up(a2, 4)
        a4 = a3 + _up(a3, 8)
        dz_ref[...] = (_pick_group(g, [a1, a2, a3, a4]) - dd).astype(BF16)

        @pl.when(s == 0)
        def _():
            dw_ref[...] = jnp.zeros_like(dw_ref)
            dsc_ref[...] = jnp.zeros_like(dsc_ref)

        dw = lax.dot_general(d_, dq, (((0,), (0,)), ((), ())), preferred_element_type=F32)
        for k in range(4):
            dw_ref[k] += dw[k * PGs:(k + 1) * PGs, :]
        dsc_ref[...] += jnp.sum(dy_ * q, axis=0, keepdims=True)

    return pl.pallas_call(
        body, name=name, grid=(G, T // L),
        in_specs=[pl.BlockSpec((L, PG), lambda g, s: (s, g)), pl.BlockSpec((L, PG), lambda g, s: (s, off + g)),
                  pl.BlockSpec((None, PG, PG), lambda g, s: (g, 0, 0)), pl.BlockSpec((1, PG), lambda g, s: (0, g))],
        out_specs=[pl.BlockSpec((L, PG), lambda g, s: (s, g)),
                   pl.BlockSpec((4, None, PGs, PG), lambda g, s: (0, g, 0, 0)),
                   pl.BlockSpec((1, PG), lambda g, s: (0, g))],
        out_shape=[jax.ShapeDtypeStruct((T, G * PG), BF16), jax.ShapeDtypeStruct((4, G, PGs, PG), F32),
                   jax.ShapeDtypeStruct((1, G * PG), F32)],
        compiler_params=_params(),
    )(d, dy, pool_w, pool_scale)


def _input_grad(dr1, dh_branch, L, name, after=None):
    T, D = dr1.shape
    nseq = T // L
    tc = _tile(D, (512, 256, 128))

    def body(a_ref, b_ref, gx_ref, gm_ref):
        dh = ALPHA * a_ref[...] + b_ref[...]
        gm_ref[...] = dh[:N_META]
        gx_ref[...] = dh[N_META:]

    blk = pl.BlockSpec((L, tc), lambda s, j: (s, j))
    return _call(
        body, (dr1, dh_branch), after, name=name, grid=(nseq, D // tc), in_specs=[blk, blk],
        out_specs=[pl.BlockSpec((None, L - N_META, tc), lambda s, j: (s, 0, j)),
                   pl.BlockSpec((None, N_META, tc), lambda s, j: (s, 0, j))],
        out_shape=[jax.ShapeDtypeStruct((nseq, L - N_META, D), F32), jax.ShapeDtypeStruct((nseq, N_META, D), F32)],
        compiler_params=_params(),
    )


def _chip():
    return 2 * lax.axis_index("x") + lax.axis_index("y")


def _core():
    return lax.axis_index("c")


def _cast_into_full(w, sh, name):
    tr = _rows_tile(sh.srows, sh.scols)
    nb = sh.srows // tr

    def body(w_ref, o_ref):
        o_ref[...] = w_ref[...].astype(BF16)

    if sh.axis == 0:
        out = pl.BlockSpec((tr, sh.scols), lambda i: (_chip() * nb + i, 0))
    else:
        out = pl.BlockSpec((tr, sh.scols), lambda i: (i, _chip()))
    return pl.pallas_call(
        body, name=name, grid=(nb,), in_specs=[pl.BlockSpec((tr, sh.scols), lambda i: (i, 0))], out_specs=out,
        out_shape=jax.ShapeDtypeStruct((sh.rows, sh.cols), BF16), compiler_params=_params(),
    )(w)


def _add_halves(dw, theirs, sh, name):
    R, C = sh.half_all_shape
    tr = _rows_tile(sh.h, C)
    nh, ns = sh.h // tr, sh.srows // tr

    def body(a_ref, b_ref, o_ref):
        o_ref[...] = (a_ref[...].astype(F32) + b_ref[...].astype(F32)).astype(BF16)

    blk = pl.BlockSpec((tr, C), lambda i: (i, 0))
    mine = pl.BlockSpec((tr, C), lambda i: ((i // nh) * ns + _core() * nh + i % nh, 0))
    return pl.pallas_call(body, name=name, grid=(R // tr,), in_specs=[mine, blk], out_specs=blk,
                          out_shape=jax.ShapeDtypeStruct((R, C), BF16), compiler_params=_params())(dw, theirs)


def _sum_parts(part, recv, sh, name):
    h, C = sh.half_shape
    tr = _rows_tile(h, C)
    nh = h // tr

    def body(o_ref, r_ref, out_ref):
        acc = o_ref[...].astype(F32)
        for k in range(3):
            acc = acc + r_ref[k].astype(F32)
        out_ref[...] = acc

    if sh.axis == 0:
        own = pl.BlockSpec((tr, C), lambda i: (_chip() * nh + i, 0))
    else:
        own = pl.BlockSpec((tr, C), lambda i: (i, _chip()))
    if sh.col_halves:
        out = pl.BlockSpec((tr, C), lambda i: (i, _core()))
    else:
        out = pl.BlockSpec((tr, C), lambda i: (_core() * nh + i, 0))
    return pl.pallas_call(
        body, name=name, grid=(nh,), in_specs=[own, pl.BlockSpec((3, tr, C), lambda i: (0, i, 0))], out_specs=out,
        out_shape=jax.ShapeDtypeStruct((sh.srows, sh.scols), F32), compiler_params=_params(),
    )(part, recv)


def _adamw(w, g, m, v, name, copy_grad=False):
    R, C = w.shape
    tr = _rows_tile(R, C, mult=8)

    def body(w_ref, g_ref, m_ref, v_ref, d_ref, nm_ref, nv_ref, *g_out):
        g_ = g_ref[...]
        for o in g_out:
            o[...] = g_
        m_ = ADAM_B1 * m_ref[...] + (1.0 - ADAM_B1) * g_
        v_ = ADAM_B2 * v_ref[...] + (1.0 - ADAM_B2) * (g_ * g_)
        m_hat = m_ / (1.0 - ADAM_B1 ** ADAM_STEP)
        v_hat = v_ / (1.0 - ADAM_B2 ** ADAM_STEP)
        d_ref[...] = -ADAM_LR * (m_hat / (jnp.sqrt(v_hat) + ADAM_EPS) + ADAM_WD * w_ref[...])
        nm_ref[...] = m_
        nv_ref[...] = v_

    blk = pl.BlockSpec((tr, C), lambda i: (i, 0))
    shp = jax.ShapeDtypeStruct((R, C), F32)
    n_out = 4 if copy_grad else 3
    return pl.pallas_call(body, name=name, grid=(R // tr,), in_specs=[blk] * 4, out_specs=[blk] * n_out,
                          out_shape=[shp] * n_out, compiler_params=_params())(w, g, m, v)


def _pos():
    return lax.axis_index("x"), lax.axis_index("y"), lax.axis_index("c")


def _other_chips(x, y):
    return [(1 - x, y), (x, 1 - y), (1 - x, 1 - y)]


class _Sharded:
    def __init__(self, full_shape, axis, col_halves=False):
        self.rows, self.cols = full_shape
        self.axis = axis
        self.srows = self.rows // 4 if axis == 0 else self.rows
        self.scols = self.cols if axis == 0 else self.cols // 4
        assert self.srows % 2 == 0
        self.h = self.srows // 2
        self.col_halves = col_halves
        assert not (col_halves and axis == 1)
        self.hc = self.cols // 2
        if col_halves:
            self.half_all_shape, self.half_shape = (self.rows, self.hc), (self.srows, self.hc)
        else:
            self.half_all_shape = (4 * self.h, self.cols) if axis == 0 else (self.h, self.cols)
            self.half_shape = (self.h, self.scols)

    def shard(self, ref, q):
        if self.axis == 0:
            return ref.at[pl.ds(q * self.srows, self.srows), :]
        return ref.at[:, pl.ds(pl.multiple_of(q * self.scols, LANES), self.scols)]

    def half(self, ref, q, c):
        if self.axis == 0:
            return ref.at[pl.ds(q * self.srows + c * self.h, self.h), :]
        return ref.at[pl.ds(c * self.h, self.h), pl.ds(pl.multiple_of(q * self.scols, LANES), self.scols)]

    def half_in_stack(self, ref, q):
        if self.col_halves:
            return ref.at[pl.ds(q * self.srows, self.srows), :]
        if self.axis == 0:
            return ref.at[pl.ds(q * self.h, self.h), :]
        return ref.at[:, pl.ds(pl.multiple_of(q * self.scols, LANES), self.scols)]


class _RowBatch:
    def __init__(self, sh, b):
        assert sh.axis == 1 and sh.rows % 4 == 0
        self.sh, self.b, self.h = sh, b, sh.rows // 4

    def half(self, ref, q, c):
        cols = pl.ds(pl.multiple_of(q * self.sh.scols, LANES), self.sh.scols)
        return ref.at[pl.ds((2 * self.b + c) * self.h, self.h), cols]


ANY = pl.BlockSpec(memory_space=pl.ANY)
DMA_ROW_ALIGN = 16
CHUNK_COUNTS = (16, 8, 43, 4, 2)


def _row_chunks(rows):
    n = 1
    if rows % DMA_ROW_ALIGN == 0:
        n = _tile(rows // DMA_ROW_ALIGN, CHUNK_COUNTS)
        n = n if n in CHUNK_COUNTS else 1
    size = rows // n
    return [(k * size, size) for k in range(n)]


class _Pieces:
    def __init__(self, src, dst, send_sem, recv_sem, to):
        self.args = (src, dst, send_sem, recv_sem, to)

    def _copy(self, rows=None):
        src, dst, send_sem, recv_sem, to = self.args
        if rows is not None:
            src, dst = src.at[pl.ds(*rows), :], dst.at[pl.ds(*rows), :]
        return pltpu.make_async_remote_copy(src_ref=src, dst_ref=dst, send_sem=send_sem, recv_sem=recv_sem,
                                            device_id=to, device_id_type=MESH)

    def start(self):
        for rows in _row_chunks(self.args[0].shape[0]):
            self._copy(rows).start()

    def wait_send(self):
        self._copy().wait_send()

    def wait_recv(self):
        self._copy().wait_recv()

    def wait(self):
        self._copy().wait()


HBM = pl.BlockSpec(memory_space=pltpu.HBM)
SEM = pl.BlockSpec(memory_space=pltpu.SEMAPHORE)
SPLIT_COPY = pltpu.CompilerParams(has_side_effects=pltpu.SideEffectType.DATAFLOW_SIDE_EFFECTING)
TOKEN = jax.ShapeDtypeStruct((8, LANES), F32)


def _in_hbm(a):
    return pltpu.with_memory_space_constraint(a, pltpu.HBM)


def _split_start(arrays, plan, n, name):
    na = len(arrays)

    def body(*refs):
        send_sems, recv_sems, token = refs[na], refs[na + 1], refs[-1]
        x, y, c = _pos()
        for j, (src, dst, to, _) in enumerate(plan(refs[:na], x, y, c)):
            _Pieces(src, dst, send_sems.at[j], recv_sems.at[j], to).start()
        token[...] = jnp.zeros_like(token)

    outs = pl.pallas_call(
        body, name=name, in_specs=(HBM,) * na,
        out_shape=(pltpu.SemaphoreType.DMA((n,)), pltpu.SemaphoreType.DMA((n,)),
                   *[pltpu.HBM(a.shape, a.dtype) for a in arrays], TOKEN),
        out_specs=(SEM, SEM) + (HBM,) * na + (pl.BlockSpec(memory_space=pltpu.VMEM),),
        input_output_aliases={i: 2 + i for i in range(na)}, compiler_params=SPLIT_COPY,
    )(*[_in_hbm(a) for a in arrays])
    return (outs[0], outs[1], list(outs[2:2 + na])), outs[-1]


def _split_wait(state, after, plan, name):
    send_sems, recv_sems, arrays = state
    na = len(arrays)

    def body(*refs):
        send_sems, recv_sems = refs[na], refs[na + 1]
        x, y, c = _pos()
        for j, (src, _, to, landing) in enumerate(plan(refs[:na], x, y, c)):
            cp = pltpu.make_async_remote_copy(src_ref=src, dst_ref=landing, send_sem=send_sems.at[j],
                                              recv_sem=recv_sems.at[j], device_id=to, device_id_type=MESH)
            cp.wait_send()
            cp.wait_recv()

    outs = pl.pallas_call(
        body, name=name, in_specs=(HBM,) * na + (SEM, SEM, ANY),
        out_shape=tuple(pltpu.HBM(a.shape, a.dtype) for a in arrays), out_specs=(HBM,) * na,
        input_output_aliases={i: i for i in range(na)}, compiler_params=SPLIT_COPY,
    )(*arrays, send_sems, recv_sems, after)
    return list(outs)


def _plan_gather_ici(sh):
    def plan(refs, x, y, c):
        (full,) = refs
        mine = sh.half(full, 2 * x + y, c)
        return [(mine, mine, (cx, cy, c), sh.half(full, 2 * cx + cy, c)) for cx, cy in _other_chips(x, y)]
    return plan


def _plan_gather_d2d(sh):
    def plan(refs, x, y, c):
        (full,) = refs
        out = []
        for cx, cy in _other_chips(x, y):
            landed = sh.half(full, 2 * cx + cy, c)
            out.append((landed, landed, (x, y, 1 - c), sh.half(full, 2 * cx + cy, 1 - c)))
        return out
    return plan


def _plan_rs_sibling(sh):
    def plan(refs, x, y, c):
        dw, theirs = refs
        if sh.axis == 1:
            return [(dw.at[pl.ds((1 - c) * sh.h, sh.h), :], theirs, (x, y, 1 - c), theirs)]
        out = []
        for k in range(4):
            give = dw.at[pl.ds(k * sh.srows + (1 - c) * sh.h, sh.h), :]
            theirs_k = theirs.at[pl.ds(k * sh.h, sh.h), :]
            out.append((give, theirs_k, (x, y, 1 - c), theirs_k))
        return out
    return plan


def _plan_rs_ici(sh):
    def plan(refs, x, y, c):
        part, land = refs
        return [(sh.half_in_stack(part, 2 * cx + cy), land.at[j], (cx, cy, c), land.at[j])
                for j, (cx, cy) in enumerate(_other_chips(x, y))]
    return plan


def _plan_share(sh):
    def plan(refs, x, y, c):
        (g,) = refs
        if sh.col_halves:
            mine = g.at[:, pl.ds(pl.multiple_of(c * sh.hc, LANES), sh.hc)]
            theirs = g.at[:, pl.ds(pl.multiple_of((1 - c) * sh.hc, LANES), sh.hc)]
        else:
            mine, theirs = g.at[pl.ds(c * sh.h, sh.h), :], g.at[pl.ds((1 - c) * sh.h, sh.h), :]
        return [(mine, mine, (x, y, 1 - c), theirs)]
    return plan


def _plan_to_sibling(refs, x, y, c):
    mine, theirs = refs
    return [(mine, theirs, (x, y, 1 - c), theirs)]


def _exchange_small(v, reduce, name, after=None):
    R, C = v.shape

    def body(v_ref, o_ref, buf, send_sems, recv_sems):
        x, y, c = _pos()
        me = 4 * x + 2 * y + c
        slots = buf if reduce else o_ref
        slots[me] = v_ref[...]
        cps = []
        for k in range(1, 8):
            px = 1 - x if k & 4 else x
            py = 1 - y if k & 2 else y
            pc = 1 - c if k & 1 else c
            cps.append((_Pieces(v_ref, slots.at[me], send_sems.at[k - 1], recv_sems.at[k - 1], (px, py, pc)),
                        4 * px + 2 * py + pc))
        for cp, _ in cps:
            cp.start()
        for k, (cp, peer) in enumerate(cps):
            pltpu.make_async_remote_copy(
                src_ref=v_ref, dst_ref=slots.at[peer], send_sem=send_sems.at[k], recv_sem=recv_sems.at[k],
                device_id=(x, y, c), device_id_type=MESH).wait_recv()
        if reduce:
            acc = buf[0]
            for d in range(1, 8):
                acc = acc + buf[d]
            o_ref[...] = acc
        for cp, _ in cps:
            cp.wait_send()

    vm = pl.BlockSpec(memory_space=pltpu.VMEM)
    out_shape = jax.ShapeDtypeStruct((R, C) if reduce else (8, R, C), v.dtype)
    scratch = [pltpu.VMEM((8, R, C) if reduce else (8, LANES), v.dtype),
               pltpu.SemaphoreType.DMA((7,)), pltpu.SemaphoreType.DMA((7,))]
    return _call(body, (v,), after, name=name, in_specs=[vm], out_specs=vm, out_shape=out_shape,
                 scratch_shapes=scratch, compiler_params=_params())


def _pack_rows(parts, width):
    rows, offs, at = [], [], 0
    for p in parts:
        flat = p.reshape(-1)
        n = 8 * -(-flat.shape[0] // (8 * width))
        flat = jnp.pad(flat, (0, n * width - flat.shape[0]))
        rows.append(flat.reshape(n, width))
        offs.append(at)
        at += n
    return jnp.concatenate(rows, axis=0), offs


def _unpack_rows(packed, offs, shapes):
    out = []
    for off, shp in zip(offs, shapes):
        size = 1
        for s in shp:
            size *= s
        n = -(-size // packed.shape[1])
        out.append(packed[off:off + n].reshape(-1)[:size].reshape(shp))
    return out


def kernel(x, meta_tokens, ffn1_w_gu, ffn1_w_down, ln1_g, ln1_b, w_in, conv_w, pool_w, pool_scale, w_out, ln2_g, ln2_b, ffn2_w_gu, ffn2_w_down, ln3_g, ln3_b, loss_target, m_meta_tokens, m_ffn1_w_gu, m_ffn1_w_down, m_ln1_g, m_ln1_b, m_w_in, m_conv_w, m_pool_w, m_pool_scale, m_w_out, m_ln2_g, m_ln2_b, m_ffn2_w_gu, m_ffn2_w_down, m_ln3_g, m_ln3_b, v_meta_tokens, v_ffn1_w_gu, v_ffn1_w_down, v_ln1_g, v_ln1_b, v_w_in, v_conv_w, v_pool_w, v_pool_scale, v_w_out, v_ln2_g, v_ln2_b, v_ffn2_w_gu, v_ffn2_w_down, v_ln3_g, v_ln3_b):
    nseq, S, D = x.shape
    L = S + N_META
    T = nseq * L
    F = ffn1_w_down.shape[1] * 4
    CC = conv_w.shape[2] * 4
    G, PGs, PG = pool_w.shape[1:]
    PC = G * PG
    IN = w_in.shape[2] * 4
    qx, qy, qc = _pos()
    q = 2 * qx + qy

    sh_gu = _Sharded((D, 2 * F), 1)
    sh_down = _Sharded((F, D), 0, col_halves=True)
    sh_in = _Sharded((D, IN), 1)
    sh_out = _Sharded((CC + PC, D), 0, col_halves=True)
    sh_pool = _Sharded((4 * G * PGs, PG), 0)

    gathers = {}
    tok = jnp.zeros((), F32)
    gu1_batches = [_RowBatch(sh_gu, 0), _RowBatch(sh_gu, 1)]
    gu1_buf, gu1_sems = [_cast_into_full(ffn1_w_gu[0], sh_gu, "cast_gu1")], []
    for b, batch in enumerate(gu1_batches):
        (send_sems, recv_sems, gu1_buf), t_ = _split_start(gu1_buf, _plan_gather_ici(batch), 3, "ag_start_gu1%d" % b)
        gu1_sems.append((send_sems, recv_sems))
        tok = tok + t_[0, 0]

    def gu1_batch(b, after):
        batch = gu1_batches[b]
        buf = _split_wait((*gu1_sems[b], gu1_buf), after, _plan_gather_ici(batch), "ag_wait_gu1%d" % b)
        state, t_ = _split_start(buf, _plan_gather_d2d(batch), 3, "ag_pass_gu1%d" % b)
        gu1_buf[:] = _split_wait(state, t_, _plan_gather_d2d(batch), "ag_got_gu1%d" % b)
        return gu1_buf[0]

    for tag, w, sh in (("d1", ffn1_w_down, sh_down), ("in", w_in, sh_in),
                       ("out", w_out, sh_out), ("gu2", ffn2_w_gu, sh_gu), ("d2", ffn2_w_down, sh_down)):
        full = _cast_into_full(w[0], sh, "cast_" + tag)
        state, t_ = _split_start([full], _plan_gather_ici(sh), 3, "ag_start_" + tag)
        gathers[tag] = (state, sh)
        tok = tok + t_[0, 0]

    def landed(tag, after):
        state, sh = gathers.pop(tag)
        (full,) = _split_wait(state, after, _plan_gather_ici(sh), "ag_wait_" + tag)
        state, t_ = _split_start([full], _plan_gather_d2d(sh), 3, "ag_pass_" + tag)
        gathers[tag] = (state, sh)
        return t_

    def arrive(tag, after):
        state, sh = gathers.pop(tag)
        return _split_wait(state, after, _plan_gather_d2d(sh), "ag_got_" + tag)[0]

    small_w = [meta_tokens, conv_w[0], pool_w[0]]
    packed, offs = _pack_rows(small_w, 4 * LANES)
    slots = _exchange_small(packed, False, "ag_small")
    per_chip = [_unpack_rows(slots[2 * k], offs, [p.shape for p in small_w]) for k in range(4)]
    meta_full = jnp.concatenate([p[0] for p in per_chip], axis=1)
    conv_full = jnp.concatenate([p[1] for p in per_chip], axis=1)
    poolw_full = jnp.concatenate([p[2] for p in per_chip], axis=1).astype(BF16)
    pscale = pool_scale

    h0 = jnp.concatenate([jnp.broadcast_to(meta_full[None], (nseq, N_META, D)), x], axis=1).reshape(T, D)
    h0b = (h0 + tok).astype(BF16)
    tgt = jnp.pad(loss_target, ((0, 0), (N_META, 0), (0, 0))).reshape(T, D)
    mask = (lax.broadcasted_iota(jnp.int32, (nseq, L, 1), 1) >= N_META).astype(F32).reshape(T, 1)

    gu1_half = _mm_nn(h0b, gu1_batch(0, h0b), F32, "ffn_gu", k_half=0)
    wgu1 = gu1_batch(1, gu1_half)
    gu1 = _mm_nn(h0b, wgu1, BF16, "ffn_gu", k_half=1, addend=gu1_half)
    a1 = _silu_mul(gu1, "silu_mul", after=landed("d1", gu1))
    wd1 = arrive("d1", a1)
    f1 = _mm_nn(a1, wd1, F32, "ffn_down")
    h1, h1b, xh1, rs1 = _ln_fwd(h0, f1, ln1_g, ln1_b, 0.5, "ln_fwd", after=landed("in", f1))
    win = arrive("in", h1b)

    u = _mm_nn(h1b, win, F32, "mix_in")
    yc = _conv_fwd(u, conv_full, L, CC, "conv_fwd", after=landed("out", u))
    yp, dpool = _pool_fwd(u, poolw_full, pscale, L, CC, PG, "pool_fwd")
    ymix = jnp.concatenate([yc, yp], axis=1)
    wout = arrive("out", ymix)
    o = _mm_nn(ymix, wout, F32, "mix_out")
    h2, h2b, xh2, rs2 = _ln_fwd(h1, o, ln2_g, ln2_b, 1.0, "ln_fwd", after=landed("gu2", o))
    wgu2 = arrive("gu2", h2b)

    gu2 = _mm_nn(h2b, wgu2, BF16, "ffn_gu")
    a2 = _silu_mul(gu2, "silu_mul", after=landed("d2", gu2))
    wd2 = arrive("d2", a2)
    f2 = _mm_nn(a2, wd2, F32, "ffn_down")
    dr3, dr3b, gs3, bs3, lsum = _ln3_loss(h2, f2, tgt, mask, ln3_g, ln3_b, "ln3_loss")
    loss = lax.psum(0.5 * jnp.sum(lsum) / D, ALL_AXES)

    sibs, icis, shares = {}, {}, {}

    def sib_begin(dw, sh, tag):
        theirs = lax.empty(sh.half_all_shape, dw.dtype)
        state, t_ = _split_start([dw, theirs], _plan_rs_sibling(sh), 4 if sh.axis == 0 else 1, "rs_sib_" + tag)
        sibs[tag] = (state, sh)
        return t_

    def ici_begin(tag, after):
        state, sh = sibs.pop(tag)
        dw, theirs = _split_wait(state, after, _plan_rs_sibling(sh), "rs_sibw_" + tag)
        part = _add_halves(dw, theirs, sh, "rs_add_" + tag)
        land = lax.empty((3,) + sh.half_shape, part.dtype)
        state, t_ = _split_start([part, land], _plan_rs_ici(sh), 3, "rs_start_" + tag)
        icis[tag] = (state, sh)
        return t_

    def share_begin(tag, after):
        state, sh = icis.pop(tag)
        part, recv = _split_wait(state, after, _plan_rs_ici(sh), "rs_wait_" + tag)
        g = _sum_parts(part, recv, sh, "rs_sum_" + tag)
        state, t_ = _split_start([g], _plan_share(sh), 1, "rs_share_" + tag)
        shares[tag] = (state, sh)
        return t_

    def share_end(tag, after):
        state, sh = shares.pop(tag)
        return _split_wait(state, after, _plan_share(sh), "rs_got_" + tag)[0]

    def dw_theirs(a, b, sh, tag, name, after=None):
        half = _mm_tn(a, b, name, after=after, sh=sh, which="theirs")
        state, t_ = _split_start([half, lax.empty(half.shape, half.dtype)], _plan_to_sibling, 1, "rs_sib_" + tag)
        sibs[tag] = state
        return t_

    def dw_mine(a, b, sh, tag, name, after):
        theirs = _split_wait(sibs.pop(tag), after, _plan_to_sibling, "rs_sibw_" + tag)[1]
        part = _mm_tn(a, b, name, sh=sh, which="mine", addend=theirs)
        land = lax.empty((3,) + sh.half_shape, part.dtype)
        state, t_ = _split_start([part, land], _plan_rs_ici(sh), 3, "rs_start_" + tag)
        icis[tag] = (state, sh)
        return t_

    def ffn_bwd(drb, a, gu, hb, wgu, wd, sfx):
        da = _mm_nt(drb, wd, BF16, "ffn_da")
        dgu = _swiglu_bwd(da, gu, "swiglu_bwd", after=dw_theirs(a, drb, sh_down, "d" + sfx, "ffn_dwd"))
        t_ = dw_mine(a, drb, sh_down, "d" + sfx, "ffn_dwd", dgu)
        t_ = dw_theirs(hb, dgu, sh_gu, "gu" + sfx, "ffn_dwgu", after=t_)
        return _mm_nt(dgu, wgu, F32, "ffn_dh", after=t_,
                      mid=lambda part: dw_mine(hb, dgu, sh_gu, "gu" + sfx, "ffn_dwgu", part))

    dh2 = ffn_bwd(dr3b, a2, gu2, h2b, wgu2, wd2, "2")

    dr2, dr2b, gs2, bs2 = _ln_bwd(dr3, dh2, xh2, rs2, ln2_g, 1.0, "ln_bwd")
    dymix = _mm_nt(dr2b, wout, F32, "mix_dy")
    db_, dc_, dx_, dconvw = _conv_bwd(u, dymix, conv_full, L, CC, "conv_bwd",
                                      after=dw_theirs(ymix, dr2b, sh_out, "out", "mix_dwout"))
    dz_, dpoolw, dpscale = _pool_bwd(dpool, dymix, poolw_full, pscale, L, CC, PG, "pool_bwd")
    t_sp = sib_begin(dpoolw.reshape(4 * G * PGs, PG).astype(BF16), sh_pool, "pool")
    du = jnp.concatenate([db_, dc_, dx_, dz_], axis=1)
    t_ = dw_mine(ymix, dr2b, sh_out, "out", "mix_dwout", du)
    t_ = dw_theirs(h1b, du, sh_in, "in", "mix_dwin", after=t_ + t_sp)
    dh1 = _mm_nt(du, win, F32, "mix_dh", after=t_ + ici_begin("pool", du))

    dr1, dr1b, gs1, bs1 = _ln_bwd(dr2, dh1, xh1, rs1, ln1_g, 0.5, "ln_bwd",
                                  after=dw_mine(h1b, du, sh_in, "in", "mix_dwin", dh1))
    dh0f = ffn_bwd(dr1b, a1, gu1, h0b, wgu1, wd1, "1")
    grad_x, gmeta = _input_grad(dr1, dh0f, L, "input_grad")

    names = ["meta_tokens", "ffn1_w_gu", "ffn1_w_down", "ln1_g", "ln1_b", "w_in", "conv_w", "pool_w", "pool_scale",
             "w_out", "ln2_g", "ln2_b", "ffn2_w_gu", "ffn2_w_down", "ln3_g", "ln3_b"]
    ws = dict(zip(names, [meta_tokens, ffn1_w_gu, ffn1_w_down, ln1_g, ln1_b, w_in, conv_w, pool_w, pool_scale, w_out,
                          ln2_g, ln2_b, ffn2_w_gu, ffn2_w_down, ln3_g, ln3_b]))
    ms = dict(zip(names, [m_meta_tokens, m_ffn1_w_gu, m_ffn1_w_down, m_ln1_g, m_ln1_b, m_w_in, m_conv_w, m_pool_w,
                          m_pool_scale, m_w_out, m_ln2_g, m_ln2_b, m_ffn2_w_gu, m_ffn2_w_down, m_ln3_g, m_ln3_b]))
    vs = dict(zip(names, [v_meta_tokens, v_ffn1_w_gu, v_ffn1_w_down, v_ln1_g, v_ln1_b, v_w_in, v_conv_w, v_pool_w,
                          v_pool_scale, v_w_out, v_ln2_g, v_ln2_b, v_ffn2_w_gu, v_ffn2_w_down, v_ln3_g, v_ln3_b]))
    grads, delta, new_m, new_v = {}, {}, {}, {}

    weight_of = {"d2": "ffn2_w_down", "gu2": "ffn2_w_gu", "out": "w_out", "pool": "pool_w", "in": "w_in",
                 "d1": "ffn1_w_down", "gu1": "ffn1_w_gu"}
    big = tuple(n for n in weight_of.values() if n != "pool_w")

    def finish(tag, after):
        n = weight_of[tag]
        g = share_end(tag, after)
        if n == "pool_w":
            grads[n] = g.reshape(1, G, PGs, PG)
            return g
        d_, m_, v_, g_ = _adamw(ws[n][0], g, ms[n][0], vs[n][0], "adamw_" + tag, copy_grad=True)
        grads[n], delta[n], new_m[n], new_v[n] = g_[None], d_[None], m_[None], v_[None]
        return d_

    last, prev = None, grad_x
    for tag in weight_of:
        t_ = share_begin(tag, prev)
        prev = t_ if last is None else finish(last, t_)
        last = tag

    small_g = [gs1, bs1, gs2, bs2, gs3, bs3, dpscale, dconvw, jnp.sum(gmeta, axis=0)]
    gpacked, goffs = _pack_rows(small_g, D)
    gsum = _exchange_small(gpacked, True, "ar_small", after=prev)
    (grads["ln1_g"], grads["ln1_b"], grads["ln2_g"], grads["ln2_b"], grads["ln3_g"], grads["ln3_b"],
     grads["pool_scale"], g_conv_all, g_meta_all) = _unpack_rows(gsum, goffs, [p.shape for p in small_g])
    grads["meta_tokens"] = lax.dynamic_slice_in_dim(g_meta_all, q * (D // 4), D // 4, axis=1)
    grads["conv_w"] = lax.dynamic_slice_in_dim(g_conv_all, q * (CC // 4), CC // 4, axis=1)[None]
    finish(last, gsum)

    small = [n for n in names if n not in big]
    width = 4 * LANES
    pw, poffs = _pack_rows([ws[n] for n in small], width)
    pg, _ = _pack_rows([grads[n] for n in small], width)
    pm, _ = _pack_rows([ms[n] for n in small], width)
    pv, _ = _pack_rows([vs[n] for n in small], width)
    sd, sm, sv = _adamw(pw, pg, pm, pv, "adamw_small")
    shapes = [ws[n].shape for n in small]
    for n, d_, m_, v_ in zip(small, _unpack_rows(sd, poffs, shapes), _unpack_rows(sm, poffs, shapes),
                             _unpack_rows(sv, poffs, shapes)):
        delta[n], new_m[n], new_v[n] = d_, m_, v_

    return (loss, grad_x, *[grads[n] for n in names], *[delta[n] for n in names], *[new_m[n] for n in names],
            *[new_v[n] for n in names])
```
